```python
import jax, jax.numpy as jnp
from jax import lax
import numpy as np

D_MODEL = 2048
BATCH = 2
SEQ = 16384
DEPTH = 1

D_MIX = D_MODEL
D_RWKV = D_MIX // 2
D_RET = D_MIX - D_RWKV
RWKV_HEAD = 64
RWKV_HEADS = D_RWKV // RWKV_HEAD
RET_HEADS = 4
RET_HEAD = D_RET // RET_HEADS
RET_CHUNK = 128
LORA_DECAY = 64
LORA_A = 64
LORA_GATE = 128
D_FF = 4 * D_MODEL
ROPE_BASE = 10000.0
LN_EPS = 1e-5
RWKV_GN_EPS = 64e-5
RET_GN_EPS = 1e-6
ALPHA = (2.0 * DEPTH) ** 0.25
BETA = (8.0 * DEPTH) ** -0.25

RWKV_SHIFT_WIDTH = 3 * D_RWKV + LORA_DECAY + LORA_A + LORA_GATE
N_IN = RWKV_SHIFT_WIDTH + 4 * D_RET
RWKV_SPLITS = (D_RWKV, 2 * D_RWKV, 3 * D_RWKV, 3 * D_RWKV + LORA_DECAY,
               3 * D_RWKV + LORA_DECAY + LORA_A)
RET_SPLITS = (D_RET, 2 * D_RET, 3 * D_RET)

kernel_name = "hymba_rwkv7_retnet_deepnorm"


def _layer_norm(x, w, b):
    xf = x.astype(jnp.float32)
    mu = xf.mean(-1, keepdims=True)
    var = jnp.square(xf - mu).mean(-1, keepdims=True)
    return ((xf - mu) * lax.rsqrt(var + LN_EPS) * w + b).astype(x.dtype)


def _head_norm(y, eps):
    mu = y.mean(-1, keepdims=True)
    var = jnp.square(y - mu).mean(-1, keepdims=True)
    return (y - mu) * lax.rsqrt(var + eps)


def _token_shift(f, mu):
    prev = jnp.pad(f, ((0, 0), (1, 0), (0, 0)))[:, :-1]
    return f + (prev - f) * mu


def _wkv7_scan(r, decay, k, v, a_vec, b_vec):
    B, T, H, N = r.shape

    def step(S, inp):
        r_t, w_t, k_t, v_t, a_t, b_t = inp
        sa = jnp.einsum('bhvk,bhk->bhv', S, a_t)
        S = (S * w_t[:, :, None, :] + sa[..., None] * b_t[:, :, None, :]
             + v_t[..., None] * k_t[:, :, None, :])
        y_t = jnp.einsum('bhvk,bhk->bhv', S, r_t)
        return S, y_t

    xs = tuple(jnp.moveaxis(t, 1, 0) for t in (r, decay, k, v, a_vec, b_vec))
    S0 = jnp.zeros((B, H, N, N), jnp.float32)
    _, y = lax.scan(step, S0, xs)
    return jnp.moveaxis(y, 0, 1)


def _rwkv7_mix(feat, mu_shift, w0, w_lora_up, a0, a_lora_up, g_lora_up,
               k_k, k_a, r_k, gn_w, gn_b):
    B, T, _ = feat.shape
    f = _token_shift(feat.astype(jnp.float32), mu_shift)
    r, k, v, lw, la, lg = jnp.split(f, RWKV_SPLITS, axis=-1)
    w = -jax.nn.softplus(-(w0 + jnp.tanh(lw) @ w_lora_up)) - 0.5
    a = jax.nn.sigmoid(a0 + la @ a_lora_up)
    g = jax.nn.sigmoid(lg) @ g_lora_up

    def heads(t):
        return t.reshape(B, T, RWKV_HEADS, RWKV_HEAD)

    kk = heads(k * k_k)
    kk = kk / jnp.maximum(jnp.sqrt(jnp.sum(kk * kk, -1, keepdims=True)), 1e-12)
    k = k * (1.0 + (a - 1.0) * k_a)
    r, k, v, a = heads(r), heads(k), heads(v), heads(a)
    decay = jnp.exp(-jnp.exp(heads(w)))
    y = _wkv7_scan(r, decay, k, v, -kk, kk * a)
    y = _head_norm(y, RWKV_GN_EPS) * gn_w.reshape(RWKV_HEADS, RWKV_HEAD) \
        + gn_b.reshape(RWKV_HEADS, RWKV_HEAD)
    bonus = jnp.sum(r * k * r_k, -1, keepdims=True) * v
    return (y + bonus).reshape(B, T, D_RWKV) * g


def _rotary(t, cos, sin):
    t1, t2 = t[..., 0::2], t[..., 1::2]
    return jnp.stack([t1 * cos - t2 * sin, t1 * sin + t2 * cos], axis=-1).reshape(t.shape)


def _retention_mix(q, k, v, g, gn_w):
    B, T, _ = q.shape
    nC = T // RET_CHUNK

    def heads(t):
        return t.astype(jnp.float32).reshape(B, T, RET_HEADS, RET_HEAD).transpose(0, 2, 1, 3)

    q, k, v = heads(q), heads(k), heads(v)
    pos = jnp.arange(T, dtype=jnp.float32)
    inv_freq = 1.0 / (ROPE_BASE ** jnp.linspace(0.0, 1.0, RET_HEAD // 2, dtype=jnp.float32))
    theta = pos[:, None] * inv_freq[None, :]
    cos, sin = jnp.cos(theta), jnp.sin(theta)
    q = _rotary(q, cos, sin)
    k = _rotary(k, cos, sin) * RET_HEAD ** -0.5

    log_gamma = jnp.log(1.0 - 2.0 ** (-5.0 - jnp.arange(RET_HEADS, dtype=jnp.float32)))
    idx = jnp.arange(RET_CHUNK, dtype=jnp.float32)
    rel = idx[:, None] - idx[None, :]
    decay_mask = jnp.where(rel >= 0,
                           jnp.exp(jnp.maximum(rel, 0.0) * log_gamma[:, None, None]), 0.0)
    q_decay = jnp.exp((idx + 1.0) * log_gamma[:, None])
    k_decay = jnp.exp((RET_CHUNK - 1.0 - idx) * log_gamma[:, None])
    chunk_decay = jnp.exp(RET_CHUNK * log_gamma)

    def chunks(t):
        return t.reshape(B, RET_HEADS, nC, RET_CHUNK, RET_HEAD)

    qc, kc, vc = chunks(q), chunks(k), chunks(v)
    scores = jnp.einsum('bhncd,bhnmd->bhncm', qc, kc) * decay_mask[:, None]
    intra = jnp.einsum('bhncm,bhnme->bhnce', scores, vc)

    def step(R, inp):
        q_n, k_n, v_n = inp
        cross = jnp.einsum('bhcd,bhde->bhce', q_n, R) * q_decay[None, :, :, None]
        R = R * chunk_decay[None, :, None, None] + jnp.einsum(
            'bhcd,bhce->bhde', k_n * k_decay[None, :, :, None], v_n)
        return R, cross

    xs = (jnp.moveaxis(qc, 2, 0), jnp.moveaxis(kc, 2, 0), jnp.moveaxis(vc, 2, 0))
    R0 = jnp.zeros((B, RET_HEADS, RET_HEAD, RET_HEAD), jnp.float32)
    _, cross = lax.scan(step, R0, xs)
    y = intra + jnp.moveaxis(cross, 0, 2)
    y = y.reshape(B, RET_HEADS, T, RET_HEAD).transpose(0, 2, 1, 3)
    y = _head_norm(y, RET_GN_EPS) * gn_w.reshape(RET_HEADS, RET_HEAD)
    return y.reshape(B, T, D_RET) * jax.nn.silu(g.astype(jnp.float32))


def _hybrid_layer(x, w_in, mu_shift, w0, w_lora_up, a0, a_lora_up, g_lora_up,
                  k_k, k_a, r_k, rwkv_gn_w, rwkv_gn_b, ret_gn_w, w_o,
                  ln1_w, ln1_b, w_up, w_down, ln2_w, ln2_b):
    h = x @ w_in
    y_a = _rwkv7_mix(h[..., :RWKV_SHIFT_WIDTH], mu_shift, w0, w_lora_up, a0,
                     a_lora_up, g_lora_up, k_k, k_a, r_k, rwkv_gn_w, rwkv_gn_b)
    q, k, v, g = jnp.split(h[..., RWKV_SHIFT_WIDTH:], RET_SPLITS, axis=-1)
    y_b = _retention_mix(q, k, v, g, ret_gn_w)
    mix = jnp.concatenate([y_a, y_b], axis=-1).astype(x.dtype) @ w_o
    x = _layer_norm(ALPHA * x + mix, ln1_w, ln1_b)
    ff = jnp.square(jax.nn.relu(x @ w_up)) @ w_down
    return _layer_norm(ALPHA * x + ff, ln2_w, ln2_b)


def setup_inputs(seed: int = 0) -> dict:
    key = jax.random.key(seed)
    ks = jax.random.split(key, 24)
    f32 = jnp.float32

    def nrm(k, shape, scale):
        return jax.random.normal(k, shape, f32) * scale

    L = DEPTH
    col_scale = jnp.concatenate([
        jnp.ones((2 * D_RWKV,), f32), jnp.full((D_RWKV,), BETA, f32),
        jnp.ones((LORA_DECAY + LORA_A + LORA_GATE + 2 * D_RET,), f32),
        jnp.full((D_RET,), BETA, f32), jnp.ones((D_RET,), f32)])
    ratio = jnp.arange(D_RWKV, dtype=f32) / (D_RWKV - 1)
    return {
        "x": nrm(ks[0], (BATCH, SEQ, D_MODEL), 1.0),
        "w_in": nrm(ks[1], (L, D_MODEL, N_IN), D_MODEL ** -0.5) * col_scale,
        "mu_shift": jax.random.uniform(ks[2], (L, RWKV_SHIFT_WIDTH), f32, 0.1, 0.9),
        "w0": jnp.broadcast_to(-6.5 + 5.0 * ratio ** 0.85, (L, D_RWKV)) + nrm(ks[3], (L, D_RWKV), 0.01),
        "w_lora_up": nrm(ks[4], (L, LORA_DECAY, D_RWKV), 0.1 * LORA_DECAY ** -0.5),
        "a0": nrm(ks[5], (L, D_RWKV), 0.01),
        "a_lora_up": nrm(ks[6], (L, LORA_A, D_RWKV), 0.5 * LORA_A ** -0.5),
        "g_lora_up": nrm(ks[7], (L, LORA_GATE, D_RWKV), LORA_GATE ** -0.5),
        "k_k": 0.85 + nrm(ks[8], (L, D_RWKV), 0.02),
        "k_a": 1.0 + nrm(ks[9], (L, D_RWKV), 0.02),
        "r_k": nrm(ks[10], (L, RWKV_HEADS, RWKV_HEAD), 0.1),
        "rwkv_gn_w": 1.0 + nrm(ks[11], (L, D_RWKV), 0.02),
        "rwkv_gn_b": nrm(ks[12], (L, D_RWKV), 0.02),
        "ret_gn_w": 1.0 + nrm(ks[13], (L, D_RET), 0.02),
        "w_o": nrm(ks[14], (L, D_MIX, D_MODEL), BETA * D_MIX ** -0.5),
        "ln1_w": 1.0 + nrm(ks[15], (L, D_MODEL), 0.02),
        "ln1_b": nrm(ks[16], (L, D_MODEL), 0.02),
        "w_up": nrm(ks[17], (L, D_MODEL, D_FF), BETA * D_MODEL ** -0.5),
        "w_down": nrm(ks[18], (L, D_FF, D_MODEL), BETA * D_FF ** -0.5),
        "ln2_w": 1.0 + nrm(ks[19], (L, D_MODEL), 0.02),
        "ln2_b": nrm(ks[20], (L, D_MODEL), 0.02),
    }


def reference(x, w_in, mu_shift, w0, w_lora_up, a0, a_lora_up, g_lora_up,
              k_k, k_a, r_k, rwkv_gn_w, rwkv_gn_b, ret_gn_w, w_o,
              ln1_w, ln1_b, w_up, w_down, ln2_w, ln2_b):
    for l in range(DEPTH):
        x = _hybrid_layer(x, w_in[l], mu_shift[l], w0[l], w_lora_up[l], a0[l],
                          a_lora_up[l], g_lora_up[l], k_k[l], k_a[l], r_k[l],
                          rwkv_gn_w[l], rwkv_gn_b[l], ret_gn_w[l], w_o[l],
                          ln1_w[l], ln1_b[l], w_up[l], w_down[l], ln2_w[l], ln2_b[l])
    return x
```

```python
import functools
import math

import jax
import jax.numpy as jnp
from jax import lax
from jax.experimental import pallas as pl
from jax.experimental.pallas import tpu as pltpu

F32 = jnp.float32
BF16 = jnp.bfloat16

D_MODEL = 2048
D_RWKV = 1024
D_RET = 1024
RWKV_HEAD = 64
RET_HEADS = 4
RET_HEAD = 256
RET_CHUNK = 128
LORA_DECAY = 64
LORA_A = 64
LORA_GATE = 128
D_FF = 4 * D_MODEL
ROPE_BASE = 10000.0
LN_EPS = 1e-5
RWKV_GN_EPS = 64e-5
RET_GN_EPS = 1e-6
DEPTH = 1
ALPHA = (2.0 * DEPTH) ** 0.25

RWKV_WIDTH = 3 * D_RWKV + LORA_DECAY + LORA_A + LORA_GATE
RET_WIDTH = 4 * D_RET
LANES = 128
HEAD_PAIRS = D_RWKV // LANES
WKV_CHUNK = 64
INV_BLOCK = 16
VMEM_LIMIT = 56 * 1024 * 1024


def _mm(a, b):
    return jnp.dot(a.astype(BF16), b.astype(BF16), preferred_element_type=F32)


def _mm_nt(a, b):
    return lax.dot_general(a.astype(BF16), b.astype(BF16), (((1,), (1,)), ((), ())),
                           preferred_element_type=F32)


def _mm_tn(a, b):
    return lax.dot_general(a.astype(BF16), b.astype(BF16), (((0,), (0,)), ((), ())),
                           preferred_element_type=F32)


def _mm_split(a, b):
    hi = a.astype(BF16)
    lo = (a - hi.astype(F32)).astype(BF16)
    return (jnp.dot(hi, b, preferred_element_type=F32)
            + jnp.dot(lo, b, preferred_element_type=F32))


def _layer_norm(z, w, b):
    mu = jnp.mean(z, axis=-1, keepdims=True)
    d = z - mu
    var = jnp.mean(d * d, axis=-1, keepdims=True)
    return d * lax.rsqrt(var + LN_EPS) * w + b


def _in_proj_kernel(x_ref, wa_ref, wb_ref, ha_ref, hb_ref):
    xb = x_ref[...].astype(BF16)
    ha_ref[...] = jnp.dot(xb, wa_ref[...], preferred_element_type=F32)
    hb_ref[...] = jnp.dot(xb, wb_ref[...], preferred_element_type=F32)


def _in_proj(x2d, w_a, w_b, tm):
    m = x2d.shape[0]
    return pl.pallas_call(
        _in_proj_kernel,
        grid=(m // tm,),
        in_specs=[
            pl.BlockSpec((tm, D_MODEL), lambda i: (i, 0)),
            pl.BlockSpec((D_MODEL, RWKV_WIDTH), lambda i: (0, 0), pipeline_mode=pl.Buffered(1)),
            pl.BlockSpec((D_MODEL, RET_WIDTH), lambda i: (0, 0), pipeline_mode=pl.Buffered(1)),
        ],
        out_specs=[
            pl.BlockSpec((tm, RWKV_WIDTH), lambda i: (i, 0)),
            pl.BlockSpec((tm, RET_WIDTH), lambda i: (i, 0)),
        ],
        out_shape=[
            jax.ShapeDtypeStruct((m, RWKV_WIDTH), F32),
            jax.ShapeDtypeStruct((m, RET_WIDTH), F32),
        ],
        compiler_params=pltpu.CompilerParams(
            dimension_semantics=("parallel",), vmem_limit_bytes=VMEM_LIMIT),
        name="in_proj",
    )(x2d, w_a, w_b)


def _rope_kernel(cos_ref, sin_ref, *, rows):
    i = pl.program_id(0)
    half = RET_HEAD // 2
    lane = lax.broadcasted_iota(jnp.int32, (rows, half), 1).astype(F32)
    pos = (lax.broadcasted_iota(jnp.int32, (rows, half), 0) + i * rows).astype(F32)
    inv_freq = jnp.exp(lane * (-math.log(ROPE_BASE) / (half - 1)))
    theta = pos * inv_freq
    cos_ref[...] = jnp.cos(theta)
    sin_ref[...] = jnp.sin(theta)


def _rope_table(t, rows):
    half = RET_HEAD // 2
    return pl.pallas_call(
        functools.partial(_rope_kernel, rows=rows),
        grid=(t // rows,),
        out_specs=[pl.BlockSpec((rows, half), lambda i: (i, 0)),
                   pl.BlockSpec((rows, half), lambda i: (i, 0))],
        out_shape=[jax.ShapeDtypeStruct((t, half), F32),
                   jax.ShapeDtypeStruct((t, half), F32)],
        name="rope_table",
    )()


def _token_shift(x_ref, prev_ref, mu_ref):
    x = x_ref[0]
    rows = x.shape[0]
    rolled = pltpu.roll(x, 1, axis=0)
    row = lax.broadcasted_iota(jnp.int32, x.shape, 0)
    prev = jnp.where(row == 0, prev_ref[...], rolled)
    prev_ref[...] = x[rows - 1:rows, :]
    return x + (prev - x) * mu_ref[...]


def _ipm(a, b):
    return a + b + _mm(a, b)


def _wkv_kernel(r_ref, k_ref, v_ref, lo_ref, mur_ref, muk_ref, muv_ref, mulo_ref,
                w0_ref, a0_ref, kk_ref, ka_ref, rk_ref, gnw_ref, gnb_ref,
                wd_ref, wa_ref, wg_ref, y_ref,
                st_ref, pr_ref, pk_ref, pv_ref, plo_ref, *, tb):
    L = WKV_CHUNK
    S = 2 * L

    @pl.when(pl.program_id(2) == 0)
    def _():
        st_ref[...] = jnp.zeros_like(st_ref)
        pr_ref[...] = jnp.zeros_like(pr_ref)
        pk_ref[...] = jnp.zeros_like(pk_ref)
        pv_ref[...] = jnp.zeros_like(pv_ref)
        plo_ref[...] = jnp.zeros_like(plo_ref)

    rf = _token_shift(r_ref, pr_ref, mur_ref)
    kf = _token_shift(k_ref, pk_ref, muk_ref)
    vf = _token_shift(v_ref, pv_ref, muv_ref)
    lof = _token_shift(lo_ref, plo_ref, mulo_ref)

    x1 = lof[:, :LANES]
    dec = w0_ref[...] + _mm(jnp.tanh(x1), wd_ref[...])
    z = -dec
    w = -(jnp.maximum(z, 0.0) + jnp.log(1.0 + jnp.exp(-jnp.abs(z)))) - 0.5
    logw = -jnp.exp(w)
    a = jax.nn.sigmoid(a0_ref[...] + _mm(x1, wa_ref[...]))
    g = _mm(jax.nn.sigmoid(lof[:, LANES:]), wg_ref[...])

    li = lax.broadcasted_iota(jnp.int32, (LANES, LANES), 0) // RWKV_HEAD
    lj = lax.broadcasted_iota(jnp.int32, (LANES, LANES), 1) // RWKV_HEAD
    head_ones = (li == lj).astype(BF16)

    def head_sum(x):
        return _mm_split(x, head_ones)

    kkr = kf * kk_ref[...]
    kkn = kkr / jnp.maximum(jnp.sqrt(head_sum(kkr * kkr)), 1e-12)
    k2 = kf * (1.0 + (a - 1.0) * ka_ref[...])
    bonus = head_sum(rf * k2 * rk_ref[...]) * vf
    avec = -kkn
    bvec = kkn * a

    ri = lax.broadcasted_iota(jnp.int32, (S, S), 0)
    ci = lax.broadcasted_iota(jnp.int32, (S, S), 1)
    same_head = (ri // L) == (ci // L)
    strict = same_head & (ri > ci)
    incl = same_head & (ri >= ci)
    same_blk = (ri // INV_BLOCK) == (ci // INV_BLOCK)
    lane_head = lax.broadcasted_iota(jnp.int32, (L, LANES), 1) // RWKV_HEAD
    ti = lax.broadcasted_iota(jnp.int32, (L, L), 0)
    tj = lax.broadcasted_iota(jnp.int32, (L, L), 1)
    tri = (ti >= tj).astype(BF16)

    def stack(x):
        return jnp.concatenate([jnp.where(lane_head == 0, x, 0.0),
                                jnp.where(lane_head == 1, x, 0.0)], axis=0)

    st = st_ref[...]
    ys = []
    for c in range(tb // L):
        sl = slice(c * L, (c + 1) * L)
        lw = logw[sl]
        cum = _mm_split_lhs(tri, lw)
        tot = cum[L - 1:L, :]
        e_pos = jnp.exp(cum)
        e_prev = jnp.exp(cum - lw)
        e_neg = jnp.exp(-cum)
        e_tail = jnp.exp(tot - cum)
        a_s = stack(avec[sl] * e_prev)
        r_s = stack(rf[sl] * e_pos)
        b_s = stack(bvec[sl] * e_neg)
        k_s = stack(k2[sl] * e_neg)
        v_s = stack(vf[sl])
        bh_s = stack(bvec[sl] * e_tail)
        kh_s = stack(k2[sl] * e_tail)

        big = _mm_nt(jnp.concatenate([a_s, r_s], axis=0), jnp.concatenate([b_s, k_s], axis=0))
        n_ab = jnp.where(strict, big[:S, :S], 0.0)
        a_ak = jnp.where(strict, big[:S, S:], 0.0)
        a_br = jnp.where(incl, big[S:, :S], 0.0)
        a_kr = jnp.where(incl, big[S:, S:], 0.0)

        nd = jnp.where(same_blk, n_ab, 0.0)
        ne = jnp.where(same_blk, 0.0, n_ab)
        nd2 = _mm(nd, nd)
        nd4 = _mm(nd2, nd2)
        nd8 = _mm(nd4, nd4)
        dm = _ipm(_ipm(nd, nd2), _ipm(nd4, nd8))
        f1 = ne + _mm(dm, ne)
        f2 = _mm(f1, f1)
        tm1 = _ipm(_ipm(f1, f2), dm)

        zmat = jnp.concatenate([a_s, _mm(a_ak, v_s)], axis=1)
        xmat = zmat + _mm(tm1, zmat)
        zx = _mm(a_br, xmat)
        q = r_s + zx[:, :LANES]
        y_intra = zx[:, LANES:] + _mm(a_kr, v_s)
        mu_t = _mm_tn(xmat, bh_s)
        m_t = mu_t[:LANES]
        u_t = mu_t[LANES:] + _mm_tn(v_s, kh_s)

        y_st = y_intra + _mm_nt(q, st)
        ys.append(y_st[:L] + y_st[L:])
        st = st * jnp.exp(tot) + _mm(st, m_t) + u_t
    st_ref[...] = st

    y = jnp.concatenate(ys, axis=0)
    mean = head_sum(y) * (1.0 / RWKV_HEAD)
    d = y - mean
    var = head_sum(d * d) * (1.0 / RWKV_HEAD)
    yn = d * lax.rsqrt(var + RWKV_GN_EPS) * gnw_ref[...] + gnb_ref[...]
    y_ref[0] = ((yn + bonus) * g).astype(y_ref.dtype)


def _mm_split_lhs(a, b):
    hi = b.astype(BF16)
    lo = (b - hi.astype(F32)).astype(BF16)
    return (jnp.dot(a, hi, preferred_element_type=F32)
            + jnp.dot(a, lo, preferred_element_type=F32))


def _wkv7(h_a, mu, w0, a0, k_k, k_a, r_k, gn_w, gn_b, wd_pad, wa_pad, wg, tb):
    b, t, _ = h_a.shape
    col = lambda off: (lambda bi, p, ti: (bi, ti, off + p))
    vec = lambda off: (lambda bi, p, ti: (0, off + p))
    lora_blk = (3 * D_RWKV) // (2 * LANES)
    in_specs = [
        pl.BlockSpec((1, tb, LANES), col(0)),
        pl.BlockSpec((1, tb, LANES), col(HEAD_PAIRS)),
        pl.BlockSpec((1, tb, LANES), col(2 * HEAD_PAIRS)),
        pl.BlockSpec((1, tb, 2 * LANES), lambda bi, p, ti: (bi, ti, lora_blk)),
        pl.BlockSpec((1, LANES), vec(0)),
        pl.BlockSpec((1, LANES), vec(HEAD_PAIRS)),
        pl.BlockSpec((1, LANES), vec(2 * HEAD_PAIRS)),
        pl.BlockSpec((1, 2 * LANES), lambda bi, p, ti: (0, lora_blk)),
    ] + [pl.BlockSpec((1, LANES), vec(0))] * 7 + [
        pl.BlockSpec((LANES, LANES), vec(0)),
        pl.BlockSpec((LANES, LANES), vec(0)),
        pl.BlockSpec((LANES, LANES), vec(0)),
    ]
    return pl.pallas_call(
        functools.partial(_wkv_kernel, tb=tb),
        grid=(b, HEAD_PAIRS, t // tb),
        in_specs=in_specs,
        out_specs=pl.BlockSpec((1, tb, LANES), col(0)),
        out_shape=jax.ShapeDtypeStruct((b, t, D_RWKV), BF16),
        scratch_shapes=[
            pltpu.VMEM((LANES, LANES), F32),
            pltpu.VMEM((1, LANES), F32),
            pltpu.VMEM((1, LANES), F32),
            pltpu.VMEM((1, LANES), F32),
            pltpu.VMEM((1, 2 * LANES), F32),
        ],
        compiler_params=pltpu.CompilerParams(
            dimension_semantics=("parallel", "parallel", "arbitrary"),
            vmem_limit_bytes=VMEM_LIMIT),
        name="wkv7",
    )(h_a, h_a, h_a, h_a, mu, mu, mu, mu, w0, a0, k_k, k_a, r_k, gn_w, gn_b,
      wd_pad, wa_pad, wg)


def _ret_kernel(q_ref, k_ref, v_ref, g_ref, cos_ref, sin_ref, gnw_ref, y_ref, state_ref):
    C = RET_CHUNK
    half = RET_HEAD // 2
    hd = pl.program_id(1)

    @pl.when(pl.program_id(2) == 0)
    def _():
        state_ref[...] = jnp.zeros_like(state_ref)

    log_gamma = jnp.float32(math.log(1.0 - 2.0 ** -5.0))
    for i in range(1, RET_HEADS):
        log_gamma = jnp.where(hd == i, jnp.float32(math.log(1.0 - 2.0 ** (-5.0 - i))), log_gamma)

    cos = cos_ref[...]
    sin = sin_ref[...]

    def rotary(t):
        t1, t2 = t[:, :half], t[:, half:]
        return jnp.concatenate([t1 * cos - t2 * sin, t1 * sin + t2 * cos], axis=1)

    q = rotary(q_ref[0])
    k = rotary(k_ref[0]) * (RET_HEAD ** -0.5)
    v = v_ref[0]

    ri = lax.broadcasted_iota(jnp.int32, (C, C), 0)
    ci = lax.broadcasted_iota(jnp.int32, (C, C), 1)
    rel = (ri - ci).astype(F32)
    decay_mask = jnp.where(ri >= ci, jnp.exp(jnp.maximum(rel, 0.0) * log_gamma), 0.0)
    idx = lax.broadcasted_iota(jnp.int32, (C, 1), 0).astype(F32)
    q_decay = jnp.exp((idx + 1.0) * log_gamma)
    k_decay = jnp.exp((C - 1.0 - idx) * log_gamma)
    chunk_decay = jnp.exp(C * log_gamma)

    scores = _mm_nt(q, k) * decay_mask
    intra = _mm(scores, v)
    state = state_ref[...]
    cross = _mm(q, state) * q_decay
    state_ref[...] = state * chunk_decay + _mm_tn(k * k_decay, v)

    y = intra + cross
    mu = jnp.mean(y, axis=-1, keepdims=True)
    d = y - mu
    var = jnp.mean(d * d, axis=-1, keepdims=True)
    yn = d * lax.rsqrt(var + RET_GN_EPS) * gnw_ref[...]
    gate = g_ref[0]
    y_ref[0] = (yn * (gate * jax.nn.sigmoid(gate))).astype(y_ref.dtype)


def _retention(h_b, cos, sin, gn_w):
    b, t, _ = h_b.shape
    C = RET_CHUNK
    col = lambda off: (lambda bi, h, n: (bi, n, off + h))
    return pl.pallas_call(
        _ret_kernel,
        grid=(b, RET_HEADS, t // C),
        in_specs=[
            pl.BlockSpec((1, C, RET_HEAD), col(0)),
            pl.BlockSpec((1, C, RET_HEAD), col(RET_HEADS)),
            pl.BlockSpec((1, C, RET_HEAD), col(2 * RET_HEADS)),
            pl.BlockSpec((1, C, RET_HEAD), col(3 * RET_HEADS)),
            pl.BlockSpec((C, RET_HEAD // 2), lambda bi, h, n: (n, 0)),
            pl.BlockSpec((C, RET_HEAD // 2), lambda bi, h, n: (n, 0)),
            pl.BlockSpec((1, RET_HEAD), lambda bi, h, n: (0, h)),
        ],
        out_specs=pl.BlockSpec((1, C, RET_HEAD), col(0)),
        out_shape=jax.ShapeDtypeStruct((b, t, D_RET), BF16),
        scratch_shapes=[pltpu.VMEM((RET_HEAD, RET_HEAD), F32)],
        compiler_params=pltpu.CompilerParams(
            dimension_semantics=("parallel", "parallel", "arbitrary")),
        name="retention",
    )(h_b, h_b, h_b, h_b, cos, sin, gn_w)


def _out_proj_kernel(x_ref, ya_ref, yb_ref, woa_ref, wob_ref, lnw_ref, lnb_ref, o_ref):
    mix = (jnp.dot(ya_ref[...], woa_ref[...], preferred_element_type=F32)
           + jnp.dot(yb_ref[...], wob_ref[...], preferred_element_type=F32))
    o_ref[...] = _layer_norm(ALPHA * x_ref[...] + mix, lnw_ref[...], lnb_ref[...])


def _out_proj(x2d, ya, yb, wo_a, wo_b, ln_w, ln_b, tm):
    m = x2d.shape[0]
    row = lambda w: pl.BlockSpec((tm, w), lambda i: (i, 0))
    full = lambda r, c: pl.BlockSpec((r, c), lambda i: (0, 0))
    return pl.pallas_call(
        _out_proj_kernel,
        grid=(m // tm,),
        in_specs=[row(D_MODEL), row(D_RWKV), row(D_RET), full(D_RWKV, D_MODEL),
                  full(D_RET, D_MODEL), full(1, D_MODEL), full(1, D_MODEL)],
        out_specs=row(D_MODEL),
        out_shape=jax.ShapeDtypeStruct((m, D_MODEL), F32),
        compiler_params=pltpu.CompilerParams(
            dimension_semantics=("parallel",), vmem_limit_bytes=VMEM_LIMIT),
        name="out_proj",
    )(x2d, ya, yb, wo_a, wo_b, ln_w, ln_b)


def _mlp_kernel(x_ref, wu_ref, wd_ref, lnw_ref, lnb_ref, o_ref, acc_ref):
    j = pl.program_id(1)

    @pl.when(j == 0)
    def _():
        acc_ref[...] = jnp.zeros_like(acc_ref)

    hid = jnp.dot(x_ref[...].astype(BF16), wu_ref[...], preferred_element_type=F32)
    hid = jnp.square(jnp.maximum(hid, 0.0))
    acc_ref[...] += jnp.dot(hid.astype(BF16), wd_ref[...], preferred_element_type=F32)

    @pl.when(j == pl.num_programs(1) - 1)
    def _():
        o_ref[...] = _layer_norm(ALPHA * x_ref[...] + acc_ref[...], lnw_ref[...], lnb_ref[...])


def _mlp(x1, w_up, w_down, ln_w, ln_b, tm, tf):
    m = x1.shape[0]
    return pl.pallas_call(
        _mlp_kernel,
        grid=(m // tm, D_FF // tf),
        in_specs=[
            pl.BlockSpec((tm, D_MODEL), lambda i, j: (i, 0)),
            pl.BlockSpec((D_MODEL, tf), lambda i, j: (0, j)),
            pl.BlockSpec((tf, D_MODEL), lambda i, j: (j, 0)),
            pl.BlockSpec((1, D_MODEL), lambda i, j: (0, 0)),
            pl.BlockSpec((1, D_MODEL), lambda i, j: (0, 0)),
        ],
        out_specs=pl.BlockSpec((tm, D_MODEL), lambda i, j: (i, 0)),
        out_shape=jax.ShapeDtypeStruct((m, D_MODEL), F32),
        scratch_shapes=[pltpu.VMEM((tm, D_MODEL), F32)],
        compiler_params=pltpu.CompilerParams(
            dimension_semantics=("parallel", "arbitrary"), vmem_limit_bytes=VMEM_LIMIT),
        name="mlp",
    )(x1, w_up, w_down, ln_w, ln_b)


def _split_rotary_columns(w_ret):
    rows = w_ret.shape[0]
    qk = w_ret[:, :2 * D_RET].reshape(rows, 2 * RET_HEADS, RET_HEAD // 2, 2)
    qk = qk.transpose(0, 1, 3, 2).reshape(rows, 2 * D_RET)
    return jnp.concatenate([qk, w_ret[:, 2 * D_RET:]], axis=1)


def _layer(x, w_in, mu_shift, w0, w_lora_up, a0, a_lora_up, g_lora_up, k_k, k_a, r_k,
           rwkv_gn_w, rwkv_gn_b, ret_gn_w, w_o, ln1_w, ln1_b, w_up, w_down, ln2_w, ln2_b):
    b, t, _ = x.shape
    m = b * t
    x2d = x.reshape(m, D_MODEL)
    row = lambda p: p.reshape(1, -1)

    w_a = w_in[:, :RWKV_WIDTH].astype(BF16)
    w_b = _split_rotary_columns(w_in[:, RWKV_WIDTH:]).astype(BF16)
    h_a, h_b = _in_proj(x2d, w_a, w_b, tm=min(256, m))
    h_a = h_a.reshape(b, t, RWKV_WIDTH)
    h_b = h_b.reshape(b, t, RET_WIDTH)

    zeros = jnp.zeros((LORA_DECAY, D_RWKV), F32)
    wd_pad = jnp.concatenate([w_lora_up, zeros], axis=0).astype(BF16)
    wa_pad = jnp.concatenate([zeros, a_lora_up], axis=0).astype(BF16)
    y_a = _wkv7(h_a, row(mu_shift), row(w0), row(a0), row(k_k), row(k_a), row(r_k),
                row(rwkv_gn_w), row(rwkv_gn_b), wd_pad, wa_pad, g_lora_up.astype(BF16),
                tb=min(512, t))

    cos, sin = _rope_table(t, rows=min(512, t))
    y_b = _retention(h_b, cos, sin, row(ret_gn_w))

    w_o16 = w_o.astype(BF16)
    x1 = _out_proj(x2d, y_a.reshape(m, D_RWKV), y_b.reshape(m, D_RET), w_o16[:D_RWKV],
                   w_o16[D_RWKV:], row(ln1_w), row(ln1_b), tm=min(256, m))
    out = _mlp(x1, w_up.astype(BF16), w_down.astype(BF16), row(ln2_w), row(ln2_b),
               tm=min(512, m), tf=1024)
    return out.reshape(b, t, D_MODEL)


def kernel(x, w_in, mu_shift, w0, w_lora_up, a0, a_lora_up, g_lora_up, k_k, k_a, r_k,
           rwkv_gn_w, rwkv_gn_b, ret_gn_w, w_o, ln1_w, ln1_b, w_up, w_down, ln2_w, ln2_b):
    for l in range(DEPTH):
        x = _layer(x, w_in[l], mu_shift[l], w0[l], w_lora_up[l], a0[l], a_lora_up[l],
                   g_lora_up[l], k_k[l], k_a[l], r_k[l], rwkv_gn_w[l], rwkv_gn_b[l],
                   ret_gn_w[l], w_o[l], ln1_w[l], ln1_b[l], w_up[l], w_down[l],
                   ln2_w[l], ln2_b[l])
    return x
```

```python
import functools
import math

import jax
import jax.numpy as jnp
from jax import lax
from jax.experimental import pallas as pl
from jax.experimental.pallas import tpu as pltpu

F32 = jnp.float32
BF16 = jnp.bfloat16

D_MODEL = 2048
D_RWKV = 1024
D_RET = 1024
RWKV_HEAD = 64
RET_HEADS = 4
RET_HEAD = 256
RET_CHUNK = 128
LORA_DECAY = 64
LORA_A = 64
LORA_GATE = 128
D_FF = 4 * D_MODEL
ROPE_BASE = 10000.0
LN_EPS = 1e-5
RWKV_GN_EPS = 64e-5
RET_GN_EPS = 1e-6
DEPTH = 1
ALPHA = (2.0 * DEPTH) ** 0.25

RWKV_WIDTH = 3 * D_RWKV + LORA_DECAY + LORA_A + LORA_GATE
RET_WIDTH = 4 * D_RET
LANES = 128
HEAD_PAIRS = D_RWKV // LANES
WKV_CHUNK = 64
INV_BLOCK = 16
VMEM_LIMIT = 56 * 1024 * 1024


def _mm(a, b):
    return jnp.dot(a.astype(BF16), b.astype(BF16), preferred_element_type=F32)


def _mm_nt(a, b):
    return lax.dot_general(a.astype(BF16), b.astype(BF16), (((1,), (1,)), ((), ())),
                           preferred_element_type=F32)


def _mm_tn(a, b):
    return lax.dot_general(a.astype(BF16), b.astype(BF16), (((0,), (0,)), ((), ())),
                           preferred_element_type=F32)


def _mm_split(a, b):
    hi = a.astype(BF16)
    lo = (a - hi.astype(F32)).astype(BF16)
    return (jnp.dot(hi, b, preferred_element_type=F32)
            + jnp.dot(lo, b, preferred_element_type=F32))


def _layer_norm(z, w, b):
    mu = jnp.mean(z, axis=-1, keepdims=True)
    d = z - mu
    var = jnp.mean(d * d, axis=-1, keepdims=True)
    return d * lax.rsqrt(var + LN_EPS) * w + b


def _in_proj_kernel(x_ref, wa_ref, wb_ref, ha_ref, hb_ref):
    xb = x_ref[...].astype(BF16)
    ha_ref[...] = jnp.dot(xb, wa_ref[...], preferred_element_type=F32)
    hb_ref[...] = jnp.dot(xb, wb_ref[...], preferred_element_type=F32)


def _in_proj(x2d, w_a, w_b, tm):
    m = x2d.shape[0]
    return pl.pallas_call(
        _in_proj_kernel,
        grid=(m // tm,),
        in_specs=[
            pl.BlockSpec((tm, D_MODEL), lambda i: (i, 0)),
            pl.BlockSpec((D_MODEL, RWKV_WIDTH), lambda i: (0, 0), pipeline_mode=pl.Buffered(1)),
            pl.BlockSpec((D_MODEL, RET_WIDTH), lambda i: (0, 0), pipeline_mode=pl.Buffered(1)),
        ],
        out_specs=[
            pl.BlockSpec((tm, RWKV_WIDTH), lambda i: (i, 0)),
            pl.BlockSpec((tm, RET_WIDTH), lambda i: (i, 0)),
        ],
        out_shape=[
            jax.ShapeDtypeStruct((m, RWKV_WIDTH), F32),
            jax.ShapeDtypeStruct((m, RET_WIDTH), F32),
        ],
        compiler_params=pltpu.CompilerParams(
            dimension_semantics=("parallel",), vmem_limit_bytes=VMEM_LIMIT),
        name="in_proj",
    )(x2d, w_a, w_b)


def _rope_kernel(cos_ref, sin_ref, *, rows):
    i = pl.program_id(0)
    half = RET_HEAD // 2
    lane = lax.broadcasted_iota(jnp.int32, (rows, half), 1).astype(F32)
    pos = (lax.broadcasted_iota(jnp.int32, (rows, half), 0) + i * rows).astype(F32)
    inv_freq = jnp.exp(lane * (-math.log(ROPE_BASE) / (half - 1)))
    theta = pos * inv_freq
    cos_ref[...] = jnp.cos(theta)
    sin_ref[...] = jnp.sin(theta)


def _rope_table(t, rows):
    half = RET_HEAD // 2
    return pl.pallas_call(
        functools.partial(_rope_kernel, rows=rows),
        grid=(t // rows,),
        out_specs=[pl.BlockSpec((rows, half), lambda i: (i, 0)),
                   pl.BlockSpec((rows, half), lambda i: (i, 0))],
        out_shape=[jax.ShapeDtypeStruct((t, half), F32),
                   jax.ShapeDtypeStruct((t, half), F32)],
        name="rope_table",
    )()


def _token_shift(x_ref, prev_ref, mu_ref):
    x = x_ref[0]
    rows = x.shape[0]
    rolled = pltpu.roll(x, 1, axis=0)
    row = lax.broadcasted_iota(jnp.int32, x.shape, 0)
    prev = jnp.where(row == 0, prev_ref[...], rolled)
    prev_ref[...] = x[rows - 1:rows, :]
    return x + (prev - x) * mu_ref[...]


def _ipm(a, b):
    return a + b + _mm(a, b)


def _wkv_kernel(r_ref, k_ref, v_ref, lo_ref, mur_ref, muk_ref, muv_ref, mulo_ref,
                w0_ref, a0_ref, kk_ref, ka_ref, rk_ref, gnw_ref, gnb_ref,
                wd_ref, wa_ref, wg_ref, y_ref,
                st_ref, pr_ref, pk_ref, pv_ref, plo_ref, *, tb):
    L = WKV_CHUNK
    S = 2 * L

    @pl.when(pl.program_id(2) == 0)
    def _():
        st_ref[...] = jnp.zeros_like(st_ref)
        pr_ref[...] = jnp.zeros_like(pr_ref)
        pk_ref[...] = jnp.zeros_like(pk_ref)
        pv_ref[...] = jnp.zeros_like(pv_ref)
        plo_ref[...] = jnp.zeros_like(plo_ref)

    rf = _token_shift(r_ref, pr_ref, mur_ref)
    kf = _token_shift(k_ref, pk_ref, muk_ref)
    vf = _token_shift(v_ref, pv_ref, muv_ref)
    lof = _token_shift(lo_ref, plo_ref, mulo_ref)

    x1 = lof[:, :LANES]
    dec = w0_ref[...] + _mm(jnp.tanh(x1), wd_ref[...])
    z = -dec
    w = -(jnp.maximum(z, 0.0) + jnp.log(1.0 + jnp.exp(-jnp.abs(z)))) - 0.5
    logw = -jnp.exp(w)
    a = jax.nn.sigmoid(a0_ref[...] + _mm(x1, wa_ref[...]))
    g = _mm(jax.nn.sigmoid(lof[:, LANES:]), wg_ref[...])

    li = lax.broadcasted_iota(jnp.int32, (LANES, LANES), 0) // RWKV_HEAD
    lj = lax.broadcasted_iota(jnp.int32, (LANES, LANES), 1) // RWKV_HEAD
    head_ones = (li == lj).astype(BF16)

    def head_sum(x):
        return _mm_split(x, head_ones)

    kkr = kf * kk_ref[...]
    kkn = kkr / jnp.maximum(jnp.sqrt(head_sum(kkr * kkr)), 1e-12)
    k2 = kf * (1.0 + (a - 1.0) * ka_ref[...])
    bonus = head_sum(rf * k2 * rk_ref[...]) * vf
    avec = -kkn
    bvec = kkn * a

    ri = lax.broadcasted_iota(jnp.int32, (S, S), 0)
    ci = lax.broadcasted_iota(jnp.int32, (S, S), 1)
    same_head = (ri // L) == (ci // L)
    strict = same_head & (ri > ci)
    incl = same_head & (ri >= ci)
    same_blk = (ri // INV_BLOCK) == (ci // INV_BLOCK)
    lane_head = lax.broadcasted_iota(jnp.int32, (L, LANES), 1) // RWKV_HEAD
    ti = lax.broadcasted_iota(jnp.int32, (L, L), 0)
    tj = lax.broadcasted_iota(jnp.int32, (L, L), 1)
    tri = (ti >= tj).astype(BF16)

    def stack(x):
        return jnp.concatenate([jnp.where(lane_head == 0, x, 0.0),
                                jnp.where(lane_head == 1, x, 0.0)], axis=0)

    chunks = range(tb // L)
    lws = [logw[c * L:(c + 1) * L] for c in chunks]
    cums = [_mm_split_lhs(tri, lw) for lw in lws]
    tots = [cum[L - 1:L, :] for cum in cums]
    a_s, r_s, b16, k16, v16, bh16, kh16 = [], [], [], [], [], [], []
    for c in chunks:
        sl = slice(c * L, (c + 1) * L)
        cum, lw, tot = cums[c], lws[c], tots[c]
        e_neg = jnp.exp(-cum)
        e_tail = jnp.exp(tot - cum)
        a_s.append(stack(avec[sl] * jnp.exp(cum - lw)))
        r_s.append(stack(rf[sl] * jnp.exp(cum)))
        b16.append(stack(bvec[sl] * e_neg).astype(BF16))
        k16.append(stack(k2[sl] * e_neg).astype(BF16))
        v16.append(stack(vf[sl]).astype(BF16))
        bh16.append(stack(bvec[sl] * e_tail).astype(BF16))
        kh16.append(stack(k2[sl] * e_tail).astype(BF16))
    a16 = [x.astype(BF16) for x in a_s]
    bigs = [_mm_nt(jnp.concatenate([a16[c], r_s[c].astype(BF16)], axis=0),
                   jnp.concatenate([b16[c], k16[c]], axis=0)) for c in chunks]
    n_ab = [jnp.where(strict, big[:S, :S], 0.0) for big in bigs]
    a_ak = [jnp.where(strict, big[:S, S:], 0.0) for big in bigs]
    a_br = [jnp.where(incl, big[S:, :S], 0.0) for big in bigs]
    a_kr = [jnp.where(incl, big[S:, S:], 0.0) for big in bigs]

    nd = [jnp.where(same_blk, n, 0.0) for n in n_ab]
    ne = [jnp.where(same_blk, 0.0, n) for n in n_ab]
    akv = [_mm(a_ak[c], v16[c]) for c in chunks]
    krv = [_mm(a_kr[c], v16[c]) for c in chunks]
    vkh = [_mm_tn(v16[c], kh16[c]) for c in chunks]
    nd2 = [_mm(x, x) for x in nd]
    nd4 = [_mm(x, x) for x in nd2]
    p1 = [_ipm(nd[c], nd2[c]) for c in chunks]
    nd8 = [_mm(x, x) for x in nd4]
    p2 = [_ipm(nd4[c], nd8[c]) for c in chunks]
    dm = [_ipm(p1[c], p2[c]) for c in chunks]
    f1 = [ne[c] + _mm(dm[c], ne[c]) for c in chunks]
    f2 = [_mm(x, x) for x in f1]
    gm = [_ipm(f1[c], f2[c]) for c in chunks]
    tm1 = [_ipm(gm[c], dm[c]) for c in chunks]
    zmat = [jnp.concatenate([a_s[c], akv[c]], axis=1) for c in chunks]
    xmat = [zmat[c] + _mm(tm1[c], zmat[c]) for c in chunks]
    x16 = [x.astype(BF16) for x in xmat]
    zx = [_mm(a_br[c], x16[c]) for c in chunks]
    mu_t = [_mm_tn(x16[c], bh16[c]) for c in chunks]
    q = [r_s[c] + zx[c][:, :LANES] for c in chunks]
    y_intra = [zx[c][:, LANES:] + krv[c] for c in chunks]
    m_t = [mu_t[c][:LANES] for c in chunks]
    u_t = [mu_t[c][LANES:] + vkh[c] for c in chunks]
    decay = [jnp.exp(tot) for tot in tots]

    st = st_ref[...]
    ys = []
    for c in chunks:
        y_st = y_intra[c] + _mm_nt(q[c], st)
        ys.append(y_st[:L] + y_st[L:])
        st = st * decay[c] + _mm(st, m_t[c]) + u_t[c]
    st_ref[...] = st

    y = jnp.concatenate(ys, axis=0)
    mean = head_sum(y) * (1.0 / RWKV_HEAD)
    d = y - mean
    var = head_sum(d * d) * (1.0 / RWKV_HEAD)
    yn = d * lax.rsqrt(var + RWKV_GN_EPS) * gnw_ref[...] + gnb_ref[...]
    y_ref[0] = ((yn + bonus) * g).astype(y_ref.dtype)


def _mm_split_lhs(a, b):
    hi = b.astype(BF16)
    lo = (b - hi.astype(F32)).astype(BF16)
    return (jnp.dot(a, hi, preferred_element_type=F32)
            + jnp.dot(a, lo, preferred_element_type=F32))


def _wkv7(h_a, mu, w0, a0, k_k, k_a, r_k, gn_w, gn_b, wd_pad, wa_pad, wg, tb):
    b, t, _ = h_a.shape
    col = lambda off: (lambda bi, p, ti: (bi, ti, off + p))
    vec = lambda off: (lambda bi, p, ti: (0, off + p))
    lora_blk = (3 * D_RWKV) // (2 * LANES)
    in_specs = [
        pl.BlockSpec((1, tb, LANES), col(0)),
        pl.BlockSpec((1, tb, LANES), col(HEAD_PAIRS)),
        pl.BlockSpec((1, tb, LANES), col(2 * HEAD_PAIRS)),
        pl.BlockSpec((1, tb, 2 * LANES), lambda bi, p, ti: (bi, ti, lora_blk)),
        pl.BlockSpec((1, LANES), vec(0)),
        pl.BlockSpec((1, LANES), vec(HEAD_PAIRS)),
        pl.BlockSpec((1, LANES), vec(2 * HEAD_PAIRS)),
        pl.BlockSpec((1, 2 * LANES), lambda bi, p, ti: (0, lora_blk)),
    ] + [pl.BlockSpec((1, LANES), vec(0))] * 7 + [
        pl.BlockSpec((LANES, LANES), vec(0)),
        pl.BlockSpec((LANES, LANES), vec(0)),
        pl.BlockSpec((LANES, LANES), vec(0)),
    ]
    return pl.pallas_call(
        functools.partial(_wkv_kernel, tb=tb),
        grid=(b, HEAD_PAIRS, t // tb),
        in_specs=in_specs,
        out_specs=pl.BlockSpec((1, tb, LANES), col(0)),
        out_shape=jax.ShapeDtypeStruct((b, t, D_RWKV), BF16),
        scratch_shapes=[
            pltpu.VMEM((LANES, LANES), F32),
            pltpu.VMEM((1, LANES), F32),
            pltpu.VMEM((1, LANES), F32),
            pltpu.VMEM((1, LANES), F32),
            pltpu.VMEM((1, 2 * LANES), F32),
        ],
        compiler_params=pltpu.CompilerParams(
            dimension_semantics=("parallel", "parallel", "arbitrary"),
            vmem_limit_bytes=VMEM_LIMIT),
        name="wkv7",
    )(h_a, h_a, h_a, h_a, mu, mu, mu, mu, w0, a0, k_k, k_a, r_k, gn_w, gn_b,
      wd_pad, wa_pad, wg)


def _ret_kernel(q_ref, k_ref, v_ref, g_ref, cos_ref, sin_ref, gnw_ref, y_ref, state_ref):
    C = RET_CHUNK
    half = RET_HEAD // 2
    hd = pl.program_id(1)

    @pl.when(pl.program_id(2) == 0)
    def _():
        state_ref[...] = jnp.zeros_like(state_ref)

    log_gamma = jnp.float32(math.log(1.0 - 2.0 ** -5.0))
    for i in range(1, RET_HEADS):
        log_gamma = jnp.where(hd == i, jnp.float32(math.log(1.0 - 2.0 ** (-5.0 - i))), log_gamma)

    cos = cos_ref[...]
    sin = sin_ref[...]

    def rotary(t):
        t1, t2 = t[:, :half], t[:, half:]
        return jnp.concatenate([t1 * cos - t2 * sin, t1 * sin + t2 * cos], axis=1)

    q = rotary(q_ref[0])
    k = rotary(k_ref[0]) * (RET_HEAD ** -0.5)
    v = v_ref[0]

    ri = lax.broadcasted_iota(jnp.int32, (C, C), 0)
    ci = lax.broadcasted_iota(jnp.int32, (C, C), 1)
    rel = (ri - ci).astype(F32)
    decay_mask = jnp.where(ri >= ci, jnp.exp(jnp.maximum(rel, 0.0) * log_gamma), 0.0)
    idx = lax.broadcasted_iota(jnp.int32, (C, 1), 0).astype(F32)
    q_decay = jnp.exp((idx + 1.0) * log_gamma)
    k_decay = jnp.exp((C - 1.0 - idx) * log_gamma)
    chunk_decay = jnp.exp(C * log_gamma)

    scores = _mm_nt(q, k) * decay_mask
    intra = _mm(scores, v)
    state = state_ref[...]
    cross = _mm(q, state) * q_decay
    state_ref[...] = state * chunk_decay + _mm_tn(k * k_decay, v)

    y = intra + cross
    mu = jnp.mean(y, axis=-1, keepdims=True)
    d = y - mu
    var = jnp.mean(d * d, axis=-1, keepdims=True)
    yn = d * lax.rsqrt(var + RET_GN_EPS) * gnw_ref[...]
    gate = g_ref[0]
    y_ref[0] = (yn * (gate * jax.nn.sigmoid(gate))).astype(y_ref.dtype)


def _retention(h_b, cos, sin, gn_w):
    b, t, _ = h_b.shape
    C = RET_CHUNK
    col = lambda off: (lambda bi, h, n: (bi, n, off + h))
    return pl.pallas_call(
        _ret_kernel,
        grid=(b, RET_HEADS, t // C),
        in_specs=[
            pl.BlockSpec((1, C, RET_HEAD), col(0)),
            pl.BlockSpec((1, C, RET_HEAD), col(RET_HEADS)),
            pl.BlockSpec((1, C, RET_HEAD), col(2 * RET_HEADS)),
            pl.BlockSpec((1, C, RET_HEAD), col(3 * RET_HEADS)),
            pl.BlockSpec((C, RET_HEAD // 2), lambda bi, h, n: (n, 0)),
            pl.BlockSpec((C, RET_HEAD // 2), lambda bi, h, n: (n, 0)),
            pl.BlockSpec((1, RET_HEAD), lambda bi, h, n: (0, h)),
        ],
        out_specs=pl.BlockSpec((1, C, RET_HEAD), col(0)),
        out_shape=jax.ShapeDtypeStruct((b, t, D_RET), BF16),
        scratch_shapes=[pltpu.VMEM((RET_HEAD, RET_HEAD), F32)],
        compiler_params=pltpu.CompilerParams(
            dimension_semantics=("parallel", "parallel", "arbitrary")),
        name="retention",
    )(h_b, h_b, h_b, h_b, cos, sin, gn_w)


def _out_proj_kernel(x_ref, ya_ref, yb_ref, woa_ref, wob_ref, lnw_ref, lnb_ref, o_ref):
    mix = (jnp.dot(ya_ref[...], woa_ref[...], preferred_element_type=F32)
           + jnp.dot(yb_ref[...], wob_ref[...], preferred_element_type=F32))
    o_ref[...] = _layer_norm(ALPHA * x_ref[...] + mix, lnw_ref[...], lnb_ref[...])


def _out_proj(x2d, ya, yb, wo_a, wo_b, ln_w, ln_b, tm):
    m = x2d.shape[0]
    row = lambda w: pl.BlockSpec((tm, w), lambda i: (i, 0))
    full = lambda r, c: pl.BlockSpec((r, c), lambda i: (0, 0))
    return pl.pallas_call(
        _out_proj_kernel,
        grid=(m // tm,),
        in_specs=[row(D_MODEL), row(D_RWKV), row(D_RET), full(D_RWKV, D_MODEL),
                  full(D_RET, D_MODEL), full(1, D_MODEL), full(1, D_MODEL)],
        out_specs=row(D_MODEL),
        out_shape=jax.ShapeDtypeStruct((m, D_MODEL), F32),
        compiler_params=pltpu.CompilerParams(
            dimension_semantics=("parallel",), vmem_limit_bytes=VMEM_LIMIT),
        name="out_proj",
    )(x2d, ya, yb, wo_a, wo_b, ln_w, ln_b)


def _mlp_kernel(x_ref, wu_ref, wd_ref, lnw_ref, lnb_ref, o_ref, acc_ref):
    j = pl.program_id(1)

    @pl.when(j == 0)
    def _():
        acc_ref[...] = jnp.zeros_like(acc_ref)

    hid = jnp.dot(x_ref[...].astype(BF16), wu_ref[...], preferred_element_type=F32)
    hid = jnp.square(jnp.maximum(hid, 0.0))
    acc_ref[...] += jnp.dot(hid.astype(BF16), wd_ref[...], preferred_element_type=F32)

    @pl.when(j == pl.num_programs(1) - 1)
    def _():
        o_ref[...] = _layer_norm(ALPHA * x_ref[...] + acc_ref[...], lnw_ref[...], lnb_ref[...])


def _mlp(x1, w_up, w_down, ln_w, ln_b, tm, tf):
    m = x1.shape[0]
    return pl.pallas_call(
        _mlp_kernel,
        grid=(m // tm, D_FF // tf),
        in_specs=[
            pl.BlockSpec((tm, D_MODEL), lambda i, j: (i, 0)),
            pl.BlockSpec((D_MODEL, tf), lambda i, j: (0, j)),
            pl.BlockSpec((tf, D_MODEL), lambda i, j: (j, 0)),
            pl.BlockSpec((1, D_MODEL), lambda i, j: (0, 0)),
            pl.BlockSpec((1, D_MODEL), lambda i, j: (0, 0)),
        ],
        out_specs=pl.BlockSpec((tm, D_MODEL), lambda i, j: (i, 0)),
        out_shape=jax.ShapeDtypeStruct((m, D_MODEL), F32),
        scratch_shapes=[pltpu.VMEM((tm, D_MODEL), F32)],
        compiler_params=pltpu.CompilerParams(
            dimension_semantics=("parallel", "arbitrary"), vmem_limit_bytes=VMEM_LIMIT),
        name="mlp",
    )(x1, w_up, w_down, ln_w, ln_b)


def _split_rotary_columns(w_ret):
    rows = w_ret.shape[0]
    qk = w_ret[:, :2 * D_RET].reshape(rows, 2 * RET_HEADS, RET_HEAD // 2, 2)
    qk = qk.transpose(0, 1, 3, 2).reshape(rows, 2 * D_RET)
    return jnp.concatenate([qk, w_ret[:, 2 * D_RET:]], axis=1)


def _layer(x, w_in, mu_shift, w0, w_lora_up, a0, a_lora_up, g_lora_up, k_k, k_a, r_k,
           rwkv_gn_w, rwkv_gn_b, ret_gn_w, w_o, ln1_w, ln1_b, w_up, w_down, ln2_w, ln2_b):
    b, t, _ = x.shape
    m = b * t
    x2d = x.reshape(m, D_MODEL)
    row = lambda p: p.reshape(1, -1)

    w_a = w_in[:, :RWKV_WIDTH].astype(BF16)
    w_b = _split_rotary_columns(w_in[:, RWKV_WIDTH:]).astype(BF16)
    h_a, h_b = _in_proj(x2d, w_a, w_b, tm=min(256, m))
    h_a = h_a.reshape(b, t, RWKV_WIDTH)
    h_b = h_b.reshape(b, t, RET_WIDTH)

    zeros = jnp.zeros((LORA_DECAY, D_RWKV), F32)
    wd_pad = jnp.concatenate([w_lora_up, zeros], axis=0).astype(BF16)
    wa_pad = jnp.concatenate([zeros, a_lora_up], axis=0).astype(BF16)
    y_a = _wkv7(h_a, row(mu_shift), row(w0), row(a0), row(k_k), row(k_a), row(r_k),
                row(rwkv_gn_w), row(rwkv_gn_b), wd_pad, wa_pad, g_lora_up.astype(BF16),
                tb=min(512, t))

    cos, sin = _rope_table(t, rows=min(512, t))
    y_b = _retention(h_b, cos, sin, row(ret_gn_w))

    w_o16 = w_o.astype(BF16)
    x1 = _out_proj(x2d, y_a.reshape(m, D_RWKV), y_b.reshape(m, D_RET), w_o16[:D_RWKV],
                   w_o16[D_RWKV:], row(ln1_w), row(ln1_b), tm=min(256, m))
    out = _mlp(x1, w_up.astype(BF16), w_down.astype(BF16), row(ln2_w), row(ln2_b),
               tm=min(512, m), tf=1024)
    return out.reshape(b, t, D_MODEL)


def kernel(x, w_in, mu_shift, w0, w_lora_up, a0, a_lora_up, g_lora_up, k_k, k_a, r_k,
           rwkv_gn_w, rwkv_gn_b, ret_gn_w, w_o, ln1_w, ln1_b, w_up, w_down, ln2_w, ln2_b):
    for l in range(DEPTH):
        x = _layer(x, w_in[l], mu_shift[l], w0[l], w_lora_up[l], a0[l], a_lora_up[l],
                   g_lora_up[l], k_k[l], k_a[l], r_k[l], rwkv_gn_w[l], rwkv_gn_b[l],
                   ret_gn_w[l], w_o[l], ln1_w[l], ln1_b[l], w_up[l], w_down[l],
                   ln2_w[l], ln2_b[l])
    return x
```

```python
import functools
import math

import jax
import jax.numpy as jnp
from jax import lax
from jax.experimental import pallas as pl
from jax.experimental.pallas import tpu as pltpu

F32 = jnp.float32
BF16 = jnp.bfloat16

D_MODEL = 2048
D_RWKV = 1024
D_RET = 1024
RWKV_HEAD = 64
RET_HEADS = 4
RET_HEAD = 256
RET_CHUNK = 128
LORA_DECAY = 64
LORA_A = 64
LORA_GATE = 128
D_FF = 4 * D_MODEL
ROPE_BASE = 10000.0
LN_EPS = 1e-5
RWKV_GN_EPS = 64e-5
RET_GN_EPS = 1e-6
DEPTH = 1
ALPHA = (2.0 * DEPTH) ** 0.25

RWKV_WIDTH = 3 * D_RWKV + LORA_DECAY + LORA_A + LORA_GATE
RET_WIDTH = 4 * D_RET
LANES = 128
HEAD_PAIRS = D_RWKV // LANES
WKV_CHUNK = 64
INV_BLOCK = 16
VMEM_LIMIT = 56 * 1024 * 1024


def _mm(a, b):
    return jnp.dot(a.astype(BF16), b.astype(BF16), preferred_element_type=F32)


def _mm_nt(a, b):
    return lax.dot_general(a.astype(BF16), b.astype(BF16), (((1,), (1,)), ((), ())),
                           preferred_element_type=F32)


def _mm_tn(a, b):
    return lax.dot_general(a.astype(BF16), b.astype(BF16), (((0,), (0,)), ((), ())),
                           preferred_element_type=F32)


def _mm_split(a, b):
    hi = a.astype(BF16)
    lo = (a - hi.astype(F32)).astype(BF16)
    return (jnp.dot(hi, b, preferred_element_type=F32)
            + jnp.dot(lo, b, preferred_element_type=F32))


def _layer_norm(z, w, b):
    mu = jnp.mean(z, axis=-1, keepdims=True)
    d = z - mu
    var = jnp.mean(d * d, axis=-1, keepdims=True)
    return d * lax.rsqrt(var + LN_EPS) * w + b


def _in_proj_kernel(x_ref, wa_ref, wb_ref, ha_ref, hb_ref):
    xb = x_ref[...].astype(BF16)
    ha_ref[...] = jnp.dot(xb, wa_ref[...], preferred_element_type=F32)
    hb_ref[...] = jnp.dot(xb, wb_ref[...], preferred_element_type=F32)


def _in_proj(x2d, w_a, w_b, tm):
    m = x2d.shape[0]
    return pl.pallas_call(
        _in_proj_kernel,
        grid=(m // tm,),
        in_specs=[
            pl.BlockSpec((tm, D_MODEL), lambda i: (i, 0)),
            pl.BlockSpec((D_MODEL, RWKV_WIDTH), lambda i: (0, 0), pipeline_mode=pl.Buffered(1)),
            pl.BlockSpec((D_MODEL, RET_WIDTH), lambda i: (0, 0), pipeline_mode=pl.Buffered(1)),
        ],
        out_specs=[
            pl.BlockSpec((tm, RWKV_WIDTH), lambda i: (i, 0)),
            pl.BlockSpec((tm, RET_WIDTH), lambda i: (i, 0)),
        ],
        out_shape=[
            jax.ShapeDtypeStruct((m, RWKV_WIDTH), F32),
            jax.ShapeDtypeStruct((m, RET_WIDTH), F32),
        ],
        compiler_params=pltpu.CompilerParams(
            dimension_semantics=("parallel",), vmem_limit_bytes=VMEM_LIMIT),
        name="in_proj",
    )(x2d, w_a, w_b)


def _rope_kernel(cos_ref, sin_ref, *, rows):
    i = pl.program_id(0)
    half = RET_HEAD // 2
    lane = lax.broadcasted_iota(jnp.int32, (rows, half), 1).astype(F32)
    pos = (lax.broadcasted_iota(jnp.int32, (rows, half), 0) + i * rows).astype(F32)
    inv_freq = jnp.exp(lane * (-math.log(ROPE_BASE) / (half - 1)))
    theta = pos * inv_freq
    cos_ref[...] = jnp.cos(theta)
    sin_ref[...] = jnp.sin(theta)


def _rope_table(t, rows):
    half = RET_HEAD // 2
    return pl.pallas_call(
        functools.partial(_rope_kernel, rows=rows),
        grid=(t // rows,),
        out_specs=[pl.BlockSpec((rows, half), lambda i: (i, 0)),
                   pl.BlockSpec((rows, half), lambda i: (i, 0))],
        out_shape=[jax.ShapeDtypeStruct((t, half), F32),
                   jax.ShapeDtypeStruct((t, half), F32)],
        name="rope_table",
    )()


def _token_shift(x_ref, prev_ref, mu_ref):
    x = x_ref[0]
    rows = x.shape[0]
    rolled = pltpu.roll(x, 1, axis=0)
    row = lax.broadcasted_iota(jnp.int32, x.shape, 0)
    prev = jnp.where(row == 0, prev_ref[...], rolled)
    prev_ref[...] = x[rows - 1:rows, :]
    return x + (prev - x) * mu_ref[...]


def _ipm(a, b):
    return a + b + _mm(a, b)


def _wkv_kernel(r_ref, k_ref, v_ref, lo_ref, mur_ref, muk_ref, muv_ref, mulo_ref,
                w0_ref, a0_ref, kk_ref, ka_ref, rk_ref, gnw_ref, gnb_ref,
                wd_ref, wa_ref, wg_ref, y_ref,
                st_ref, pr_ref, pk_ref, pv_ref, plo_ref, *, tb, pairs):
    L = WKV_CHUNK
    S = 2 * L
    width = pairs * LANES

    @pl.when(pl.program_id(2) == 0)
    def _():
        st_ref[...] = jnp.zeros_like(st_ref)
        pr_ref[...] = jnp.zeros_like(pr_ref)
        pk_ref[...] = jnp.zeros_like(pk_ref)
        pv_ref[...] = jnp.zeros_like(pv_ref)
        plo_ref[...] = jnp.zeros_like(plo_ref)

    rf = _token_shift(r_ref, pr_ref, mur_ref)
    kf = _token_shift(k_ref, pk_ref, muk_ref)
    vf = _token_shift(v_ref, pv_ref, muv_ref)
    lof = _token_shift(lo_ref, plo_ref, mulo_ref)

    x1 = lof[:, :LANES]
    dec = w0_ref[...] + _mm(jnp.tanh(x1), wd_ref[...])
    z = -dec
    w = -(jnp.maximum(z, 0.0) + jnp.log(1.0 + jnp.exp(-jnp.abs(z)))) - 0.5
    logw = -jnp.exp(w)
    a = jax.nn.sigmoid(a0_ref[...] + _mm(x1, wa_ref[...]))
    g = _mm(jax.nn.sigmoid(lof[:, LANES:]), wg_ref[...])

    li = lax.broadcasted_iota(jnp.int32, (width, width), 0) // RWKV_HEAD
    lj = lax.broadcasted_iota(jnp.int32, (width, width), 1) // RWKV_HEAD
    head_ones = (li == lj).astype(BF16)

    def head_sum(x):
        return _mm_split(x, head_ones)

    kkr = kf * kk_ref[...]
    kkn = kkr / jnp.maximum(jnp.sqrt(head_sum(kkr * kkr)), 1e-12)
    k2 = kf * (1.0 + (a - 1.0) * ka_ref[...])
    bonus = head_sum(rf * k2 * rk_ref[...]) * vf
    avec = -kkn
    bvec = kkn * a

    ri = lax.broadcasted_iota(jnp.int32, (S, S), 0)
    ci = lax.broadcasted_iota(jnp.int32, (S, S), 1)
    same_head = (ri // L) == (ci // L)
    strict = same_head & (ri > ci)
    incl = same_head & (ri >= ci)
    same_blk = (ri // INV_BLOCK) == (ci // INV_BLOCK)
    lane_head = lax.broadcasted_iota(jnp.int32, (L, LANES), 1) // RWKV_HEAD
    ti = lax.broadcasted_iota(jnp.int32, (L, L), 0)
    tj = lax.broadcasted_iota(jnp.int32, (L, L), 1)
    tri = (ti >= tj).astype(BF16)

    def stack(x):
        return jnp.concatenate([jnp.where(lane_head == 0, x, 0.0),
                                jnp.where(lane_head == 1, x, 0.0)], axis=0)

    nc = tb // L
    units = [(c, p) for c in range(nc) for p in range(pairs)]
    chunks = range(len(units))

    def part(x, c, p):
        return x[c * L:(c + 1) * L, p * LANES:(p + 1) * LANES]

    cum_c = [_mm_split_lhs(tri, logw[c * L:(c + 1) * L]) for c in range(nc)]
    a_s, r_s, b16, k16, v16, bh16, kh16, decay = [], [], [], [], [], [], [], []
    for c, p in units:
        lw = part(logw, c, p)
        cum = cum_c[c][:, p * LANES:(p + 1) * LANES]
        tot = cum[L - 1:L, :]
        e_neg = jnp.exp(-cum)
        e_tail = jnp.exp(tot - cum)
        bv, kv = part(bvec, c, p), part(k2, c, p)
        a_s.append(stack(part(avec, c, p) * jnp.exp(cum - lw)))
        r_s.append(stack(part(rf, c, p) * jnp.exp(cum)))
        b16.append(stack(bv * e_neg).astype(BF16))
        k16.append(stack(kv * e_neg).astype(BF16))
        v16.append(stack(part(vf, c, p)).astype(BF16))
        bh16.append(stack(bv * e_tail).astype(BF16))
        kh16.append(stack(kv * e_tail).astype(BF16))
        decay.append(jnp.exp(tot))
    a16 = [x.astype(BF16) for x in a_s]
    bigs = [_mm_nt(jnp.concatenate([a16[c], r_s[c].astype(BF16)], axis=0),
                   jnp.concatenate([b16[c], k16[c]], axis=0)) for c in chunks]
    n_ab = [jnp.where(strict, big[:S, :S], 0.0) for big in bigs]
    a_ak = [jnp.where(strict, big[:S, S:], 0.0) for big in bigs]
    a_br = [jnp.where(incl, big[S:, :S], 0.0) for big in bigs]
    a_kr = [jnp.where(incl, big[S:, S:], 0.0) for big in bigs]

    nd = [jnp.where(same_blk, n, 0.0) for n in n_ab]
    ne = [jnp.where(same_blk, 0.0, n) for n in n_ab]
    akv = [_mm(a_ak[c], v16[c]) for c in chunks]
    krv = [_mm(a_kr[c], v16[c]) for c in chunks]
    vkh = [_mm_tn(v16[c], kh16[c]) for c in chunks]
    nd2 = [_mm(x, x) for x in nd]
    nd4 = [_mm(x, x) for x in nd2]
    p1 = [_ipm(nd[c], nd2[c]) for c in chunks]
    nd8 = [_mm(x, x) for x in nd4]
    p2 = [_ipm(nd4[c], nd8[c]) for c in chunks]
    dm = [_ipm(p1[c], p2[c]) for c in chunks]
    f1 = [ne[c] + _mm(dm[c], ne[c]) for c in chunks]
    f2 = [_mm(x, x) for x in f1]
    gm = [_ipm(f1[c], f2[c]) for c in chunks]
    tm1 = [_ipm(gm[c], dm[c]) for c in chunks]
    zmat = [jnp.concatenate([a_s[c], akv[c]], axis=1) for c in chunks]
    xmat = [zmat[c] + _mm(tm1[c], zmat[c]) for c in chunks]
    x16 = [x.astype(BF16) for x in xmat]
    zx = [_mm(a_br[c], x16[c]) for c in chunks]
    mu_t = [_mm_tn(x16[c], bh16[c]) for c in chunks]
    q = [r_s[c] + zx[c][:, :LANES] for c in chunks]
    y_intra = [zx[c][:, LANES:] + krv[c] for c in chunks]
    m_t = [mu_t[c][:LANES] for c in chunks]
    u_t = [mu_t[c][LANES:] + vkh[c] for c in chunks]

    st = [st_ref[p] for p in range(pairs)]
    ys = [[] for _ in range(pairs)]
    for u, (c, p) in enumerate(units):
        y_st = y_intra[u] + _mm_nt(q[u], st[p])
        ys[p].append(y_st[:L] + y_st[L:])
        st[p] = st[p] * decay[u] + _mm(st[p], m_t[u]) + u_t[u]
    for p in range(pairs):
        st_ref[p] = st[p]

    y = jnp.concatenate([jnp.concatenate(yp, axis=0) for yp in ys], axis=1)
    mean = head_sum(y) * (1.0 / RWKV_HEAD)
    d = y - mean
    var = head_sum(d * d) * (1.0 / RWKV_HEAD)
    yn = d * lax.rsqrt(var + RWKV_GN_EPS) * gnw_ref[...] + gnb_ref[...]
    y_ref[0] = ((yn + bonus) * g).astype(y_ref.dtype)


def _mm_split_lhs(a, b):
    hi = b.astype(BF16)
    lo = (b - hi.astype(F32)).astype(BF16)
    return (jnp.dot(a, hi, preferred_element_type=F32)
            + jnp.dot(a, lo, preferred_element_type=F32))


def _wkv7(h_a, mu, w0, a0, k_k, k_a, r_k, gn_w, gn_b, wd_pad, wa_pad, wg, tb, pairs):
    b, t, _ = h_a.shape
    width = pairs * LANES
    groups = D_RWKV // width
    col = lambda off: (lambda bi, p, ti: (bi, ti, off + p))
    vec = lambda off: (lambda bi, p, ti: (0, off + p))
    lora_blk = (3 * D_RWKV) // (2 * LANES)
    in_specs = [
        pl.BlockSpec((1, tb, width), col(0)),
        pl.BlockSpec((1, tb, width), col(groups)),
        pl.BlockSpec((1, tb, width), col(2 * groups)),
        pl.BlockSpec((1, tb, 2 * LANES), lambda bi, p, ti: (bi, ti, lora_blk)),
        pl.BlockSpec((1, width), vec(0)),
        pl.BlockSpec((1, width), vec(groups)),
        pl.BlockSpec((1, width), vec(2 * groups)),
        pl.BlockSpec((1, 2 * LANES), lambda bi, p, ti: (0, lora_blk)),
    ] + [pl.BlockSpec((1, width), vec(0))] * 7 + [
        pl.BlockSpec((LANES, width), vec(0)),
        pl.BlockSpec((LANES, width), vec(0)),
        pl.BlockSpec((LANES, width), vec(0)),
    ]
    return pl.pallas_call(
        functools.partial(_wkv_kernel, tb=tb, pairs=pairs),
        grid=(b, groups, t // tb),
        in_specs=in_specs,
        out_specs=pl.BlockSpec((1, tb, width), col(0)),
        out_shape=jax.ShapeDtypeStruct((b, t, D_RWKV), BF16),
        scratch_shapes=[
            pltpu.VMEM((pairs, LANES, LANES), F32),
            pltpu.VMEM((1, width), F32),
            pltpu.VMEM((1, width), F32),
            pltpu.VMEM((1, width), F32),
            pltpu.VMEM((1, 2 * LANES), F32),
        ],
        compiler_params=pltpu.CompilerParams(
            dimension_semantics=("parallel", "parallel", "arbitrary"),
            vmem_limit_bytes=VMEM_LIMIT),
        name="wkv7",
    )(h_a, h_a, h_a, h_a, mu, mu, mu, mu, w0, a0, k_k, k_a, r_k, gn_w, gn_b,
      wd_pad, wa_pad, wg)


def _ret_kernel(q_ref, k_ref, v_ref, g_ref, cos_ref, sin_ref, gnw_ref, y_ref, state_ref):
    C = RET_CHUNK
    half = RET_HEAD // 2
    hd = pl.program_id(1)

    @pl.when(pl.program_id(2) == 0)
    def _():
        state_ref[...] = jnp.zeros_like(state_ref)

    log_gamma = jnp.float32(math.log(1.0 - 2.0 ** -5.0))
    for i in range(1, RET_HEADS):
        log_gamma = jnp.where(hd == i, jnp.float32(math.log(1.0 - 2.0 ** (-5.0 - i))), log_gamma)

    cos = cos_ref[...]
    sin = sin_ref[...]

    def rotary(t):
        t1, t2 = t[:, :half], t[:, half:]
        return jnp.concatenate([t1 * cos - t2 * sin, t1 * sin + t2 * cos], axis=1)

    q_all = rotary(q_ref[0])
    k_all = rotary(k_ref[0]) * (RET_HEAD ** -0.5)
    v_all = v_ref[0].astype(BF16)

    ri = lax.broadcasted_iota(jnp.int32, (C, C), 0)
    ci = lax.broadcasted_iota(jnp.int32, (C, C), 1)
    rel = (ri - ci).astype(F32)
    decay_mask = jnp.where(ri >= ci, jnp.exp(jnp.maximum(rel, 0.0) * log_gamma), 0.0)
    idx = lax.broadcasted_iota(jnp.int32, (C, 1), 0).astype(F32)
    q_decay = jnp.exp((idx + 1.0) * log_gamma)
    k_decay = jnp.exp((C - 1.0 - idx) * log_gamma)
    chunk_decay = jnp.exp(C * log_gamma)

    chunks = range(q_all.shape[0] // C)
    rows = lambda x, c: x[c * C:(c + 1) * C]
    q = [rows(q_all, c).astype(BF16) for c in chunks]
    k = [rows(k_all, c) for c in chunks]
    v = [rows(v_all, c) for c in chunks]
    scores = [_mm_nt(q[c], k[c]) * decay_mask for c in chunks]
    kv = [_mm_tn(k[c] * k_decay, v[c]) for c in chunks]
    intra = [_mm(scores[c], v[c]) for c in chunks]
    states = [state_ref[...]]
    for c in chunks:
        states.append(states[c] * chunk_decay + kv[c])
    state_ref[...] = states[-1]
    cross = [_mm(q[c], states[c]) * q_decay for c in chunks]

    y = jnp.concatenate([intra[c] + cross[c] for c in chunks], axis=0)
    mu = jnp.mean(y, axis=-1, keepdims=True)
    d = y - mu
    var = jnp.mean(d * d, axis=-1, keepdims=True)
    yn = d * lax.rsqrt(var + RET_GN_EPS) * gnw_ref[...]
    gate = g_ref[0]
    y_ref[0] = (yn * (gate * jax.nn.sigmoid(gate))).astype(y_ref.dtype)


def _retention(h_b, cos, sin, gn_w, tb):
    b, t, _ = h_b.shape
    col = lambda off: (lambda bi, h, n: (bi, n, off + h))
    return pl.pallas_call(
        _ret_kernel,
        grid=(b, RET_HEADS, t // tb),
        in_specs=[
            pl.BlockSpec((1, tb, RET_HEAD), col(0)),
            pl.BlockSpec((1, tb, RET_HEAD), col(RET_HEADS)),
            pl.BlockSpec((1, tb, RET_HEAD), col(2 * RET_HEADS)),
            pl.BlockSpec((1, tb, RET_HEAD), col(3 * RET_HEADS)),
            pl.BlockSpec((tb, RET_HEAD // 2), lambda bi, h, n: (n, 0)),
            pl.BlockSpec((tb, RET_HEAD // 2), lambda bi, h, n: (n, 0)),
            pl.BlockSpec((1, RET_HEAD), lambda bi, h, n: (0, h)),
        ],
        out_specs=pl.BlockSpec((1, tb, RET_HEAD), col(0)),
        out_shape=jax.ShapeDtypeStruct((b, t, D_RET), BF16),
        scratch_shapes=[pltpu.VMEM((RET_HEAD, RET_HEAD), F32)],
        compiler_params=pltpu.CompilerParams(
            dimension_semantics=("parallel", "parallel", "arbitrary")),
        name="retention",
    )(h_b, h_b, h_b, h_b, cos, sin, gn_w)


def _out_proj_kernel(x_ref, ya_ref, yb_ref, woa_ref, wob_ref, lnw_ref, lnb_ref, o_ref):
    mix = (jnp.dot(ya_ref[...], woa_ref[...], preferred_element_type=F32)
           + jnp.dot(yb_ref[...], wob_ref[...], preferred_element_type=F32))
    o_ref[...] = _layer_norm(ALPHA * x_ref[...] + mix, lnw_ref[...], lnb_ref[...])


def _out_proj(x2d, ya, yb, wo_a, wo_b, ln_w, ln_b, tm):
    m = x2d.shape[0]
    row = lambda w: pl.BlockSpec((tm, w), lambda i: (i, 0))
    full = lambda r, c: pl.BlockSpec((r, c), lambda i: (0, 0))
    return pl.pallas_call(
        _out_proj_kernel,
        grid=(m // tm,),
        in_specs=[row(D_MODEL), row(D_RWKV), row(D_RET), full(D_RWKV, D_MODEL),
                  full(D_RET, D_MODEL), full(1, D_MODEL), full(1, D_MODEL)],
        out_specs=row(D_MODEL),
        out_shape=jax.ShapeDtypeStruct((m, D_MODEL), F32),
        compiler_params=pltpu.CompilerParams(
            dimension_semantics=("parallel",), vmem_limit_bytes=VMEM_LIMIT),
        name="out_proj",
    )(x2d, ya, yb, wo_a, wo_b, ln_w, ln_b)


def _mlp_kernel(x_ref, wu_ref, wd_ref, lnw_ref, lnb_ref, o_ref, acc_ref):
    j = pl.program_id(1)

    @pl.when(j == 0)
    def _():
        acc_ref[...] = jnp.zeros_like(acc_ref)

    hid = jnp.dot(x_ref[...].astype(BF16), wu_ref[...], preferred_element_type=F32)
    hid = jnp.square(jnp.maximum(hid, 0.0))
    acc_ref[...] += jnp.dot(hid.astype(BF16), wd_ref[...], preferred_element_type=F32)

    @pl.when(j == pl.num_programs(1) - 1)
    def _():
        o_ref[...] = _layer_norm(ALPHA * x_ref[...] + acc_ref[...], lnw_ref[...], lnb_ref[...])


def _mlp(x1, w_up, w_down, ln_w, ln_b, tm, tf):
    m = x1.shape[0]
    return pl.pallas_call(
        _mlp_kernel,
        grid=(m // tm, D_FF // tf),
        in_specs=[
            pl.BlockSpec((tm, D_MODEL), lambda i, j: (i, 0)),
            pl.BlockSpec((D_MODEL, tf), lambda i, j: (0, j)),
            pl.BlockSpec((tf, D_MODEL), lambda i, j: (j, 0)),
            pl.BlockSpec((1, D_MODEL), lambda i, j: (0, 0)),
            pl.BlockSpec((1, D_MODEL), lambda i, j: (0, 0)),
        ],
        out_specs=pl.BlockSpec((tm, D_MODEL), lambda i, j: (i, 0)),
        out_shape=jax.ShapeDtypeStruct((m, D_MODEL), F32),
        scratch_shapes=[pltpu.VMEM((tm, D_MODEL), F32)],
        compiler_params=pltpu.CompilerParams(
            dimension_semantics=("parallel", "arbitrary"), vmem_limit_bytes=VMEM_LIMIT),
        name="mlp",
    )(x1, w_up, w_down, ln_w, ln_b)


def _split_rotary_columns(w_ret):
    rows = w_ret.shape[0]
    qk = w_ret[:, :2 * D_RET].reshape(rows, 2 * RET_HEADS, RET_HEAD // 2, 2)
    qk = qk.transpose(0, 1, 3, 2).reshape(rows, 2 * D_RET)
    return jnp.concatenate([qk, w_ret[:, 2 * D_RET:]], axis=1)


def _layer(x, w_in, mu_shift, w0, w_lora_up, a0, a_lora_up, g_lora_up, k_k, k_a, r_k,
           rwkv_gn_w, rwkv_gn_b, ret_gn_w, w_o, ln1_w, ln1_b, w_up, w_down, ln2_w, ln2_b):
    b, t, _ = x.shape
    m = b * t
    x2d = x.reshape(m, D_MODEL)
    row = lambda p: p.reshape(1, -1)

    w_a = w_in[:, :RWKV_WIDTH].astype(BF16)
    w_b = _split_rotary_columns(w_in[:, RWKV_WIDTH:]).astype(BF16)
    h_a, h_b = _in_proj(x2d, w_a, w_b, tm=min(256, m))
    h_a = h_a.reshape(b, t, RWKV_WIDTH)
    h_b = h_b.reshape(b, t, RET_WIDTH)

    zeros = jnp.zeros((LORA_DECAY, D_RWKV), F32)
    wd_pad = jnp.concatenate([w_lora_up, zeros], axis=0).astype(BF16)
    wa_pad = jnp.concatenate([zeros, a_lora_up], axis=0).astype(BF16)
    y_a = _wkv7(h_a, row(mu_shift), row(w0), row(a0), row(k_k), row(k_a), row(r_k),
                row(rwkv_gn_w), row(rwkv_gn_b), wd_pad, wa_pad, g_lora_up.astype(BF16),
                tb=min(512, t), pairs=2)

    cos, sin = _rope_table(t, rows=min(512, t))
    y_b = _retention(h_b, cos, sin, row(ret_gn_w), tb=min(512, t))

    w_o16 = w_o.astype(BF16)
    x1 = _out_proj(x2d, y_a.reshape(m, D_RWKV), y_b.reshape(m, D_RET), w_o16[:D_RWKV],
                   w_o16[D_RWKV:], row(ln1_w), row(ln1_b), tm=min(256, m))
    out = _mlp(x1, w_up.astype(BF16), w_down.astype(BF16), row(ln2_w), row(ln2_b),
               tm=min(512, m), tf=1024)
    return out.reshape(b, t, D_MODEL)


def kernel(x, w_in, mu_shift, w0, w_lora_up, a0, a_lora_up, g_lora_up, k_k, k_a, r_k,
           rwkv_gn_w, rwkv_gn_b, ret_gn_w, w_o, ln1_w, ln1_b, w_up, w_down, ln2_w, ln2_b):
    for l in range(DEPTH):
        x = _layer(x, w_in[l], mu_shift[l], w0[l], w_lora_up[l], a0[l], a_lora_up[l],
                   g_lora_up[l], k_k[l], k_a[l], r_k[l], rwkv_gn_w[l], rwkv_gn_b[l],
                   ret_gn_w[l], w_o[l], ln1_w[l], ln1_b[l], w_up[l], w_down[l],
                   ln2_w[l], ln2_b[l])
    return x
```

```python
import functools
import math

import jax
import jax.numpy as jnp
from jax import lax
from jax.experimental import pallas as pl
from jax.experimental.pallas import tpu as pltpu

F32 = jnp.float32
BF16 = jnp.bfloat16

D_MODEL = 2048
D_RWKV = 1024
D_RET = 1024
RWKV_HEAD = 64
RET_HEADS = 4
RET_HEAD = 256
RET_CHUNK = 128
LORA_DECAY = 64
LORA_A = 64
LORA_GATE = 128
D_FF = 4 * D_MODEL
ROPE_BASE = 10000.0
LN_EPS = 1e-5
RWKV_GN_EPS = 64e-5
RET_GN_EPS = 1e-6
DEPTH = 1
ALPHA = (2.0 * DEPTH) ** 0.25

RWKV_WIDTH = 3 * D_RWKV + LORA_DECAY + LORA_A + LORA_GATE
RET_WIDTH = 4 * D_RET
LANES = 128
HEAD_PAIRS = D_RWKV // LANES
WKV_CHUNK = 64
INV_BLOCK = 16
VMEM_LIMIT = 56 * 1024 * 1024


def _mm(a, b):
    return jnp.dot(a.astype(BF16), b.astype(BF16), preferred_element_type=F32)


def _mm_nt(a, b):
    return lax.dot_general(a.astype(BF16), b.astype(BF16), (((1,), (1,)), ((), ())),
                           preferred_element_type=F32)


def _mm_tn(a, b):
    return lax.dot_general(a.astype(BF16), b.astype(BF16), (((0,), (0,)), ((), ())),
                           preferred_element_type=F32)


def _mm_split(a, b):
    hi = a.astype(BF16)
    lo = (a - hi.astype(F32)).astype(BF16)
    return (jnp.dot(hi, b, preferred_element_type=F32)
            + jnp.dot(lo, b, preferred_element_type=F32))


def _layer_norm(z, w, b):
    mu = jnp.mean(z, axis=-1, keepdims=True)
    d = z - mu
    var = jnp.mean(d * d, axis=-1, keepdims=True)
    return d * lax.rsqrt(var + LN_EPS) * w + b


def _in_proj_kernel(x_ref, wa_ref, wb_ref, ha_ref, hb_ref):
    xb = x_ref[...].astype(BF16)
    ha_ref[...] = jnp.dot(xb, wa_ref[...], preferred_element_type=F32)
    hb_ref[...] = jnp.dot(xb, wb_ref[...], preferred_element_type=F32)


def _in_proj(x2d, w_a, w_b, tm):
    m = x2d.shape[0]
    return pl.pallas_call(
        _in_proj_kernel,
        grid=(m // tm,),
        in_specs=[
            pl.BlockSpec((tm, D_MODEL), lambda i: (i, 0)),
            pl.BlockSpec((D_MODEL, RWKV_WIDTH), lambda i: (0, 0), pipeline_mode=pl.Buffered(1)),
            pl.BlockSpec((D_MODEL, RET_WIDTH), lambda i: (0, 0), pipeline_mode=pl.Buffered(1)),
        ],
        out_specs=[
            pl.BlockSpec((tm, RWKV_WIDTH), lambda i: (i, 0)),
            pl.BlockSpec((tm, RET_WIDTH), lambda i: (i, 0)),
        ],
        out_shape=[
            jax.ShapeDtypeStruct((m, RWKV_WIDTH), F32),
            jax.ShapeDtypeStruct((m, RET_WIDTH), F32),
        ],
        compiler_params=pltpu.CompilerParams(
            dimension_semantics=("parallel",), vmem_limit_bytes=VMEM_LIMIT),
        name="in_proj",
    )(x2d, w_a, w_b)


def _rope_kernel(cos_ref, sin_ref, *, rows):
    i = pl.program_id(0)
    half = RET_HEAD // 2
    lane = lax.broadcasted_iota(jnp.int32, (rows, half), 1).astype(F32)
    pos = (lax.broadcasted_iota(jnp.int32, (rows, half), 0) + i * rows).astype(F32)
    inv_freq = jnp.exp(lane * (-math.log(ROPE_BASE) / (half - 1)))
    theta = pos * inv_freq
    cos_ref[...] = jnp.cos(theta)
    sin_ref[...] = jnp.sin(theta)


def _rope_table(t, rows):
    half = RET_HEAD // 2
    return pl.pallas_call(
        functools.partial(_rope_kernel, rows=rows),
        grid=(t // rows,),
        out_specs=[pl.BlockSpec((rows, half), lambda i: (i, 0)),
                   pl.BlockSpec((rows, half), lambda i: (i, 0))],
        out_shape=[jax.ShapeDtypeStruct((t, half), F32),
                   jax.ShapeDtypeStruct((t, half), F32)],
        name="rope_table",
    )()


def _token_shift(x_ref, prev_ref, mu_ref):
    x = x_ref[0]
    rows = x.shape[0]
    rolled = pltpu.roll(x, 1, axis=0)
    row = lax.broadcasted_iota(jnp.int32, x.shape, 0)
    prev = jnp.where(row == 0, prev_ref[...], rolled)
    prev_ref[...] = x[rows - 1:rows, :]
    return x + (prev - x) * mu_ref[...]


def _ipm(a, b):
    return a + b + _mm(a, b)


def _wkv_kernel(r_ref, k_ref, v_ref, lo_ref, mur_ref, muk_ref, muv_ref, mulo_ref,
                w0_ref, a0_ref, kk_ref, ka_ref, rk_ref, gnw_ref, gnb_ref,
                wd_ref, wa_ref, wg_ref, y_ref,
                st_ref, pr_ref, pk_ref, pv_ref, plo_ref,
                q_sc, m_sc, yi_sc, u_sc, d_sc, bonus_sc, g_sc, *, tb, pairs):
    L = WKV_CHUNK
    S = 2 * L
    width = pairs * LANES

    @pl.when(pl.program_id(2) == 0)
    def _():
        for ref in (st_ref, pr_ref, pk_ref, pv_ref, plo_ref,
                    q_sc, m_sc, yi_sc, u_sc, d_sc, bonus_sc, g_sc):
            ref[...] = jnp.zeros_like(ref)

    nc = tb // L
    units = [(c, p) for c in range(nc) for p in range(pairs)]
    chunks = range(len(units))

    st = [st_ref[p] for p in range(pairs)]
    ys = [[] for _ in range(pairs)]
    pending = list(range(nc))

    def phase_b(steps=1):
        for c in pending[:steps]:
            for p in range(pairs):
                u = c * pairs + p
                y_st = yi_sc[u] + _mm_nt(q_sc[u], st[p])
                ys[p].append(y_st[:L] + y_st[L:])
                st[p] = st[p] * d_sc[u] + _mm(st[p], m_sc[u]) + u_sc[u]
        del pending[:steps]

    phase_b(2)

    rf = _token_shift(r_ref, pr_ref, mur_ref)
    kf = _token_shift(k_ref, pk_ref, muk_ref)
    vf = _token_shift(v_ref, pv_ref, muv_ref)
    lof = _token_shift(lo_ref, plo_ref, mulo_ref)

    x1 = lof[:, :LANES]
    dec = w0_ref[...] + _mm(jnp.tanh(x1), wd_ref[...])
    z = -dec
    w = -(jnp.maximum(z, 0.0) + jnp.log(1.0 + jnp.exp(-jnp.abs(z)))) - 0.5
    logw = -jnp.exp(w)
    a = jax.nn.sigmoid(a0_ref[...] + _mm(x1, wa_ref[...]))
    g = _mm(jax.nn.sigmoid(lof[:, LANES:]), wg_ref[...])

    li = lax.broadcasted_iota(jnp.int32, (width, width), 0) // RWKV_HEAD
    lj = lax.broadcasted_iota(jnp.int32, (width, width), 1) // RWKV_HEAD
    head_ones = (li == lj).astype(BF16)

    def head_sum(x):
        return _mm_split(x, head_ones)

    kkr = kf * kk_ref[...]
    kkn = kkr / jnp.maximum(jnp.sqrt(head_sum(kkr * kkr)), 1e-12)
    k2 = kf * (1.0 + (a - 1.0) * ka_ref[...])
    bonus = head_sum(rf * k2 * rk_ref[...]) * vf
    avec = -kkn
    bvec = kkn * a

    ri = lax.broadcasted_iota(jnp.int32, (S, S), 0)
    ci = lax.broadcasted_iota(jnp.int32, (S, S), 1)
    same_head = (ri // L) == (ci // L)
    strict = same_head & (ri > ci)
    incl = same_head & (ri >= ci)
    same_blk = (ri // INV_BLOCK) == (ci // INV_BLOCK)
    lane_head = lax.broadcasted_iota(jnp.int32, (L, LANES), 1) // RWKV_HEAD
    ti = lax.broadcasted_iota(jnp.int32, (L, L), 0)
    tj = lax.broadcasted_iota(jnp.int32, (L, L), 1)
    tri = (ti >= tj).astype(BF16)

    def stack(x):
        return jnp.concatenate([jnp.where(lane_head == 0, x, 0.0),
                                jnp.where(lane_head == 1, x, 0.0)], axis=0)


    def part(x, c, p):
        return x[c * L:(c + 1) * L, p * LANES:(p + 1) * LANES]

    cum_c = [_mm_split_lhs(tri, logw[c * L:(c + 1) * L]) for c in range(nc)]
    a_s, r_s, b16, k16, v16, bh16, kh16, decay = [], [], [], [], [], [], [], []
    for c, p in units:
        lw = part(logw, c, p)
        cum = cum_c[c][:, p * LANES:(p + 1) * LANES]
        tot = cum[L - 1:L, :]
        e_neg = jnp.exp(-cum)
        e_tail = jnp.exp(tot - cum)
        bv, kv = part(bvec, c, p), part(k2, c, p)
        a_s.append(stack(part(avec, c, p) * jnp.exp(cum - lw)))
        r_s.append(stack(part(rf, c, p) * jnp.exp(cum)))
        b16.append(stack(bv * e_neg).astype(BF16))
        k16.append(stack(kv * e_neg).astype(BF16))
        v16.append(stack(part(vf, c, p)).astype(BF16))
        bh16.append(stack(bv * e_tail).astype(BF16))
        kh16.append(stack(kv * e_tail).astype(BF16))
        decay.append(jnp.exp(tot))
    a16 = [x.astype(BF16) for x in a_s]
    bigs = [_mm_nt(jnp.concatenate([a16[c], r_s[c].astype(BF16)], axis=0),
                   jnp.concatenate([b16[c], k16[c]], axis=0)) for c in chunks]
    n_ab = [jnp.where(strict, big[:S, :S], 0.0) for big in bigs]
    a_ak = [jnp.where(strict, big[:S, S:], 0.0) for big in bigs]
    a_br = [jnp.where(incl, big[S:, :S], 0.0) for big in bigs]
    a_kr = [jnp.where(incl, big[S:, S:], 0.0) for big in bigs]
    phase_b()

    nd = [jnp.where(same_blk, n, 0.0) for n in n_ab]
    ne = [jnp.where(same_blk, 0.0, n) for n in n_ab]
    akv = [_mm(a_ak[c], v16[c]) for c in chunks]
    pw = [_mm(x, x) for x in nd]
    sm = [nd[c] + pw[c] for c in chunks]
    phase_b()
    for _ in range(2):
        both = [_mm(pw[c], jnp.concatenate([sm[c], pw[c]], axis=1)) for c in chunks]
        sm = [sm[c] + both[c][:, :S] for c in chunks]
        pw = [both[c][:, S:] for c in chunks]
        phase_b()
    dm = [sm[c] + _mm(pw[c], sm[c]) for c in chunks]
    phase_b()

    krv = [_mm(a_kr[c], v16[c]) for c in chunks]
    vkh = [_mm_tn(v16[c], kh16[c]) for c in chunks]
    phase_b(len(pending))
    for p in range(pairs):
        st_ref[p] = st[p]

    y = jnp.concatenate([jnp.concatenate(yp, axis=0) for yp in ys], axis=1)
    mean = head_sum(y) * (1.0 / RWKV_HEAD)
    d = y - mean
    var = head_sum(d * d) * (1.0 / RWKV_HEAD)
    yn = d * lax.rsqrt(var + RWKV_GN_EPS) * gnw_ref[...] + gnb_ref[...]
    y_ref[0] = ((yn + bonus_sc[...]) * g_sc[...]).astype(y_ref.dtype)

    f1 = [ne[c] + _mm(dm[c], ne[c]) for c in chunks]
    f12 = [f1[c] + _mm(f1[c], f1[c]) for c in chunks]
    gm = [f1[c] + _mm(f1[c], f12[c]) for c in chunks]
    tm1 = [_ipm(gm[c], dm[c]) for c in chunks]
    zmat = [jnp.concatenate([a_s[c], akv[c]], axis=1) for c in chunks]
    xmat = [zmat[c] + _mm(tm1[c], zmat[c]) for c in chunks]
    x16 = [x.astype(BF16) for x in xmat]
    zx = [_mm(a_br[c], x16[c]) for c in chunks]
    mu_t = [_mm_tn(x16[c], bh16[c]) for c in chunks]
    q = [r_s[c] + zx[c][:, :LANES] for c in chunks]
    y_intra = [zx[c][:, LANES:] + krv[c] for c in chunks]
    m_t = [mu_t[c][:LANES] for c in chunks]
    u_t = [mu_t[c][LANES:] + vkh[c] for c in chunks]

    bonus_sc[...] = bonus
    g_sc[...] = g
    for u in chunks:
        q_sc[u] = q[u].astype(BF16)
        m_sc[u] = m_t[u].astype(BF16)
        yi_sc[u] = y_intra[u]
        u_sc[u] = u_t[u]
        d_sc[u] = decay[u]


def _mm_split_lhs(a, b):
    hi = b.astype(BF16)
    lo = (b - hi.astype(F32)).astype(BF16)
    return (jnp.dot(a, hi, preferred_element_type=F32)
            + jnp.dot(a, lo, preferred_element_type=F32))


def _wkv7(h_a, mu, w0, a0, k_k, k_a, r_k, gn_w, gn_b, wd_pad, wa_pad, wg, tb, pairs):
    b, t, _ = h_a.shape
    width = pairs * LANES
    groups = D_RWKV // width
    nt = t // tb
    units = (tb // WKV_CHUNK) * pairs
    col = lambda off: (lambda bi, p, ti: (bi, jnp.minimum(ti, nt - 1), off + p))
    vec = lambda off: (lambda bi, p, ti: (0, off + p))
    lora_blk = (3 * D_RWKV) // (2 * LANES)
    in_specs = [
        pl.BlockSpec((1, tb, width), col(0)),
        pl.BlockSpec((1, tb, width), col(groups)),
        pl.BlockSpec((1, tb, width), col(2 * groups)),
        pl.BlockSpec((1, tb, 2 * LANES),
                     lambda bi, p, ti: (bi, jnp.minimum(ti, nt - 1), lora_blk)),
        pl.BlockSpec((1, width), vec(0)),
        pl.BlockSpec((1, width), vec(groups)),
        pl.BlockSpec((1, width), vec(2 * groups)),
        pl.BlockSpec((1, 2 * LANES), lambda bi, p, ti: (0, lora_blk)),
    ] + [pl.BlockSpec((1, width), vec(0))] * 7 + [
        pl.BlockSpec((LANES, width), vec(0)),
        pl.BlockSpec((LANES, width), vec(0)),
        pl.BlockSpec((LANES, width), vec(0)),
    ]
    return pl.pallas_call(
        functools.partial(_wkv_kernel, tb=tb, pairs=pairs),
        grid=(b, groups, nt + 1),
        in_specs=in_specs,
        out_specs=pl.BlockSpec((1, tb, width),
                               lambda bi, p, ti: (bi, jnp.maximum(ti - 1, 0), p)),
        out_shape=jax.ShapeDtypeStruct((b, t, D_RWKV), BF16),
        scratch_shapes=[
            pltpu.VMEM((pairs, LANES, LANES), F32),
            pltpu.VMEM((1, width), F32),
            pltpu.VMEM((1, width), F32),
            pltpu.VMEM((1, width), F32),
            pltpu.VMEM((1, 2 * LANES), F32),
            pltpu.VMEM((units, 2 * WKV_CHUNK, LANES), BF16),
            pltpu.VMEM((units, LANES, LANES), BF16),
            pltpu.VMEM((units, 2 * WKV_CHUNK, LANES), F32),
            pltpu.VMEM((units, LANES, LANES), F32),
            pltpu.VMEM((units, 1, LANES), F32),
            pltpu.VMEM((tb, width), F32),
            pltpu.VMEM((tb, width), F32),
        ],
        compiler_params=pltpu.CompilerParams(
            dimension_semantics=("parallel", "parallel", "arbitrary"),
            vmem_limit_bytes=VMEM_LIMIT),
        name="wkv7",
    )(h_a, h_a, h_a, h_a, mu, mu, mu, mu, w0, a0, k_k, k_a, r_k, gn_w, gn_b,
      wd_pad, wa_pad, wg)


def _ret_kernel(q_ref, k_ref, v_ref, g_ref, cos_ref, sin_ref, gnw_ref, y_ref, state_ref):
    C = RET_CHUNK
    half = RET_HEAD // 2
    hd = pl.program_id(1)

    @pl.when(pl.program_id(2) == 0)
    def _():
        state_ref[...] = jnp.zeros_like(state_ref)

    log_gamma = jnp.float32(math.log(1.0 - 2.0 ** -5.0))
    for i in range(1, RET_HEADS):
        log_gamma = jnp.where(hd == i, jnp.float32(math.log(1.0 - 2.0 ** (-5.0 - i))), log_gamma)

    cos = cos_ref[...]
    sin = sin_ref[...]

    def rotary(t):
        t1, t2 = t[:, :half], t[:, half:]
        return jnp.concatenate([t1 * cos - t2 * sin, t1 * sin + t2 * cos], axis=1)

    q_all = rotary(q_ref[0])
    k_all = rotary(k_ref[0]) * (RET_HEAD ** -0.5)
    v_all = v_ref[0].astype(BF16)

    ri = lax.broadcasted_iota(jnp.int32, (C, C), 0)
    ci = lax.broadcasted_iota(jnp.int32, (C, C), 1)
    rel = (ri - ci).astype(F32)
    decay_mask = jnp.where(ri >= ci, jnp.exp(jnp.maximum(rel, 0.0) * log_gamma), 0.0)
    idx = lax.broadcasted_iota(jnp.int32, (C, 1), 0).astype(F32)
    q_decay = jnp.exp((idx + 1.0) * log_gamma)
    k_decay = jnp.exp((C - 1.0 - idx) * log_gamma)
    chunk_decay = jnp.exp(C * log_gamma)

    chunks = range(q_all.shape[0] // C)
    rows = lambda x, c: x[c * C:(c + 1) * C]
    q = [rows(q_all, c).astype(BF16) for c in chunks]
    k = [rows(k_all, c) for c in chunks]
    v = [rows(v_all, c) for c in chunks]
    scores = [_mm_nt(q[c], k[c]) * decay_mask for c in chunks]
    kv = [_mm_tn(k[c] * k_decay, v[c]) for c in chunks]
    intra = [_mm(scores[c], v[c]) for c in chunks]
    states = [state_ref[...]]
    for c in chunks:
        states.append(states[c] * chunk_decay + kv[c])
    state_ref[...] = states[-1]
    cross = [_mm(q[c], states[c]) * q_decay for c in chunks]

    y = jnp.concatenate([intra[c] + cross[c] for c in chunks], axis=0)
    mu = jnp.mean(y, axis=-1, keepdims=True)
    d = y - mu
    var = jnp.mean(d * d, axis=-1, keepdims=True)
    yn = d * lax.rsqrt(var + RET_GN_EPS) * gnw_ref[...]
    gate = g_ref[0]
    y_ref[0] = (yn * (gate * jax.nn.sigmoid(gate))).astype(y_ref.dtype)


def _retention(h_b, cos, sin, gn_w, tb):
    b, t, _ = h_b.shape
    col = lambda off: (lambda bi, h, n: (bi, n, off + h))
    return pl.pallas_call(
        _ret_kernel,
        grid=(b, RET_HEADS, t // tb),
        in_specs=[
            pl.BlockSpec((1, tb, RET_HEAD), col(0)),
            pl.BlockSpec((1, tb, RET_HEAD), col(RET_HEADS)),
            pl.BlockSpec((1, tb, RET_HEAD), col(2 * RET_HEADS)),
            pl.BlockSpec((1, tb, RET_HEAD), col(3 * RET_HEADS)),
            pl.BlockSpec((tb, RET_HEAD // 2), lambda bi, h, n: (n, 0)),
            pl.BlockSpec((tb, RET_HEAD // 2), lambda bi, h, n: (n, 0)),
            pl.BlockSpec((1, RET_HEAD), lambda bi, h, n: (0, h)),
        ],
        out_specs=pl.BlockSpec((1, tb, RET_HEAD), col(0)),
        out_shape=jax.ShapeDtypeStruct((b, t, D_RET), BF16),
        scratch_shapes=[pltpu.VMEM((RET_HEAD, RET_HEAD), F32)],
        compiler_params=pltpu.CompilerParams(
            dimension_semantics=("parallel", "parallel", "arbitrary")),
        name="retention",
    )(h_b, h_b, h_b, h_b, cos, sin, gn_w)


def _out_proj_kernel(x_ref, ya_ref, yb_ref, woa_ref, wob_ref, lnw_ref, lnb_ref, o_ref):
    mix = (jnp.dot(ya_ref[...], woa_ref[...], preferred_element_type=F32)
           + jnp.dot(yb_ref[...], wob_ref[...], preferred_element_type=F32))
    o_ref[...] = _layer_norm(ALPHA * x_ref[...] + mix, lnw_ref[...], lnb_ref[...])


def _out_proj(x2d, ya, yb, wo_a, wo_b, ln_w, ln_b, tm):
    m = x2d.shape[0]
    row = lambda w: pl.BlockSpec((tm, w), lambda i: (i, 0))
    full = lambda r, c: pl.BlockSpec((r, c), lambda i: (0, 0))
    return pl.pallas_call(
        _out_proj_kernel,
        grid=(m // tm,),
        in_specs=[row(D_MODEL), row(D_RWKV), row(D_RET), full(D_RWKV, D_MODEL),
                  full(D_RET, D_MODEL), full(1, D_MODEL), full(1, D_MODEL)],
        out_specs=row(D_MODEL),
        out_shape=jax.ShapeDtypeStruct((m, D_MODEL), F32),
        compiler_params=pltpu.CompilerParams(
            dimension_semantics=("parallel",), vmem_limit_bytes=VMEM_LIMIT),
        name="out_proj",
    )(x2d, ya, yb, wo_a, wo_b, ln_w, ln_b)


def _mlp_kernel(x_ref, wu_ref, wd_ref, lnw_ref, lnb_ref, o_ref, acc_ref):
    j = pl.program_id(1)

    @pl.when(j == 0)
    def _():
        acc_ref[...] = jnp.zeros_like(acc_ref)

    hid = jnp.dot(x_ref[...].astype(BF16), wu_ref[...], preferred_element_type=F32)
    hid = jnp.square(jnp.maximum(hid, 0.0))
    acc_ref[...] += jnp.dot(hid.astype(BF16), wd_ref[...], preferred_element_type=F32)

    @pl.when(j == pl.num_programs(1) - 1)
    def _():
        o_ref[...] = _layer_norm(ALPHA * x_ref[...] + acc_ref[...], lnw_ref[...], lnb_ref[...])


def _mlp(x1, w_up, w_down, ln_w, ln_b, tm, tf):
    m = x1.shape[0]
    return pl.pallas_call(
        _mlp_kernel,
        grid=(m // tm, D_FF // tf),
        in_specs=[
            pl.BlockSpec((tm, D_MODEL), lambda i, j: (i, 0)),
            pl.BlockSpec((D_MODEL, tf), lambda i, j: (0, j)),
            pl.BlockSpec((tf, D_MODEL), lambda i, j: (j, 0)),
            pl.BlockSpec((1, D_MODEL), lambda i, j: (0, 0)),
            pl.BlockSpec((1, D_MODEL), lambda i, j: (0, 0)),
        ],
        out_specs=pl.BlockSpec((tm, D_MODEL), lambda i, j: (i, 0)),
        out_shape=jax.ShapeDtypeStruct((m, D_MODEL), F32),
        scratch_shapes=[pltpu.VMEM((tm, D_MODEL), F32)],
        compiler_params=pltpu.CompilerParams(
            dimension_semantics=("parallel", "arbitrary"), vmem_limit_bytes=VMEM_LIMIT),
        name="mlp",
    )(x1, w_up, w_down, ln_w, ln_b)


def _split_rotary_columns(w_ret):
    rows = w_ret.shape[0]
    qk = w_ret[:, :2 * D_RET].reshape(rows, 2 * RET_HEADS, RET_HEAD // 2, 2)
    qk = qk.transpose(0, 1, 3, 2).reshape(rows, 2 * D_RET)
    return jnp.concatenate([qk, w_ret[:, 2 * D_RET:]], axis=1)


def _layer(x, w_in, mu_shift, w0, w_lora_up, a0, a_lora_up, g_lora_up, k_k, k_a, r_k,
           rwkv_gn_w, rwkv_gn_b, ret_gn_w, w_o, ln1_w, ln1_b, w_up, w_down, ln2_w, ln2_b):
    b, t, _ = x.shape
    m = b * t
    x2d = x.reshape(m, D_MODEL)
    row = lambda p: p.reshape(1, -1)

    w_a = w_in[:, :RWKV_WIDTH].astype(BF16)
    w_b = _split_rotary_columns(w_in[:, RWKV_WIDTH:]).astype(BF16)
    h_a, h_b = _in_proj(x2d, w_a, w_b, tm=min(256, m))
    h_a = h_a.reshape(b, t, RWKV_WIDTH)
    h_b = h_b.reshape(b, t, RET_WIDTH)

    zeros = jnp.zeros((LORA_DECAY, D_RWKV), F32)
    wd_pad = jnp.concatenate([w_lora_up, zeros], axis=0).astype(BF16)
    wa_pad = jnp.concatenate([zeros, a_lora_up], axis=0).astype(BF16)
    y_a = _wkv7(h_a, row(mu_shift), row(w0), row(a0), row(k_k), row(k_a), row(r_k),
                row(rwkv_gn_w), row(rwkv_gn_b), wd_pad, wa_pad, g_lora_up.astype(BF16),
                tb=min(512, t), pairs=2)

    cos, sin = _rope_table(t, rows=min(512, t))
    y_b = _retention(h_b, cos, sin, row(ret_gn_w), tb=min(512, t))

    w_o16 = w_o.astype(BF16)
    x1 = _out_proj(x2d, y_a.reshape(m, D_RWKV), y_b.reshape(m, D_RET), w_o16[:D_RWKV],
                   w_o16[D_RWKV:], row(ln1_w), row(ln1_b), tm=min(256, m))
    out = _mlp(x1, w_up.astype(BF16), w_down.astype(BF16), row(ln2_w), row(ln2_b),
               tm=min(512, m), tf=1024)
    return out.reshape(b, t, D_MODEL)


def kernel(x, w_in, mu_shift, w0, w_lora_up, a0, a_lora_up, g_lora_up, k_k, k_a, r_k,
           rwkv_gn_w, rwkv_gn_b, ret_gn_w, w_o, ln1_w, ln1_b, w_up, w_down, ln2_w, ln2_b):
    for l in range(DEPTH):
        x = _layer(x, w_in[l], mu_shift[l], w0[l], w_lora_up[l], a0[l], a_lora_up[l],
                   g_lora_up[l], k_k[l], k_a[l], r_k[l], rwkv_gn_w[l], rwkv_gn_b[l],
                   ret_gn_w[l], w_o[l], ln1_w[l], ln1_b[l], w_up[l], w_down[l],
                   ln2_w[l], ln2_b[l])
    return x
```

```python
import functools
import math

import jax
import jax.numpy as jnp
from jax import lax
from jax.experimental import pallas as pl
from jax.experimental.pallas import tpu as pltpu

F32 = jnp.float32
BF16 = jnp.bfloat16

D_MODEL = 2048
D_RWKV = 1024
D_RET = 1024
RWKV_HEAD = 64
RET_HEADS = 4
RET_HEAD = 256
RET_CHUNK = 128
LORA_DECAY = 64
LORA_A = 64
LORA_GATE = 128
D_FF = 4 * D_MODEL
ROPE_BASE = 10000.0
LN_EPS = 1e-5
RWKV_GN_EPS = 64e-5
RET_GN_EPS = 1e-6
DEPTH = 1
ALPHA = (2.0 * DEPTH) ** 0.25

RWKV_WIDTH = 3 * D_RWKV + LORA_DECAY + LORA_A + LORA_GATE
RET_WIDTH = 4 * D_RET
LANES = 128
HEAD_PAIRS = D_RWKV // LANES
WKV_CHUNK = 64
INV_BLOCK = 16
VMEM_LIMIT = 56 * 1024 * 1024


def _mm(a, b):
    return jnp.dot(a.astype(BF16), b.astype(BF16), preferred_element_type=F32)


def _mm_nt(a, b):
    return lax.dot_general(a.astype(BF16), b.astype(BF16), (((1,), (1,)), ((), ())),
                           preferred_element_type=F32)


def _mm_tn(a, b):
    return lax.dot_general(a.astype(BF16), b.astype(BF16), (((0,), (0,)), ((), ())),
                           preferred_element_type=F32)


def _mm_split(a, b):
    hi = a.astype(BF16)
    lo = (a - hi.astype(F32)).astype(BF16)
    return (jnp.dot(hi, b, preferred_element_type=F32)
            + jnp.dot(lo, b, preferred_element_type=F32))


def _layer_norm(z, w, b):
    mu = jnp.mean(z, axis=-1, keepdims=True)
    d = z - mu
    var = jnp.mean(d * d, axis=-1, keepdims=True)
    return d * lax.rsqrt(var + LN_EPS) * w + b


def _in_proj_kernel(x_ref, wa_ref, wb_ref, ha_ref, hb_ref):
    xb = x_ref[...].astype(BF16)
    ha_ref[...] = jnp.dot(xb, wa_ref[...], preferred_element_type=F32)
    hb_ref[...] = jnp.dot(xb, wb_ref[...], preferred_element_type=F32)


def _in_proj(x2d, w_a, w_b, tm):
    m = x2d.shape[0]
    return pl.pallas_call(
        _in_proj_kernel,
        grid=(m // tm,),
        in_specs=[
            pl.BlockSpec((tm, D_MODEL), lambda i: (i, 0)),
            pl.BlockSpec((D_MODEL, RWKV_WIDTH), lambda i: (0, 0), pipeline_mode=pl.Buffered(1)),
            pl.BlockSpec((D_MODEL, RET_WIDTH), lambda i: (0, 0), pipeline_mode=pl.Buffered(1)),
        ],
        out_specs=[
            pl.BlockSpec((tm, RWKV_WIDTH), lambda i: (i, 0)),
            pl.BlockSpec((tm, RET_WIDTH), lambda i: (i, 0)),
        ],
        out_shape=[
            jax.ShapeDtypeStruct((m, RWKV_WIDTH), F32),
            jax.ShapeDtypeStruct((m, RET_WIDTH), F32),
        ],
        compiler_params=pltpu.CompilerParams(
            dimension_semantics=("parallel",), vmem_limit_bytes=VMEM_LIMIT),
        name="in_proj",
    )(x2d, w_a, w_b)


def _rope_kernel(cos_ref, sin_ref, *, rows):
    i = pl.program_id(0)
    half = RET_HEAD // 2
    lane = lax.broadcasted_iota(jnp.int32, (rows, half), 1).astype(F32)
    pos = (lax.broadcasted_iota(jnp.int32, (rows, half), 0) + i * rows).astype(F32)
    inv_freq = jnp.exp(lane * (-math.log(ROPE_BASE) / (half - 1)))
    theta = pos * inv_freq
    cos_ref[...] = jnp.cos(theta)
    sin_ref[...] = jnp.sin(theta)


def _rope_table(t, rows):
    half = RET_HEAD // 2
    return pl.pallas_call(
        functools.partial(_rope_kernel, rows=rows),
        grid=(t // rows,),
        out_specs=[pl.BlockSpec((rows, half), lambda i: (i, 0)),
                   pl.BlockSpec((rows, half), lambda i: (i, 0))],
        out_shape=[jax.ShapeDtypeStruct((t, half), F32),
                   jax.ShapeDtypeStruct((t, half), F32)],
        name="rope_table",
    )()


def _token_shift(x_ref, prev_ref, mu_ref):
    x = x_ref[0]
    rows = x.shape[0]
    rolled = pltpu.roll(x, 1, axis=0)
    row = lax.broadcasted_iota(jnp.int32, x.shape, 0)
    prev = jnp.where(row == 0, prev_ref[...], rolled)
    prev_ref[...] = x[rows - 1:rows, :]
    return x + (prev - x) * mu_ref[...]


def _ipm(a, b):
    return a + b + _mm(a, b)


def _wkv_kernel(r_ref, k_ref, v_ref, lo_ref, mur_ref, muk_ref, muv_ref, mulo_ref,
                w0_ref, a0_ref, kk_ref, ka_ref, rk_ref, gnw_ref, gnb_ref,
                wd_ref, wa_ref, wg_ref, y_ref,
                st_ref, pr_ref, pk_ref, pv_ref, plo_ref,
                q_sc, m_sc, yi_sc, u_sc, d_sc, bonus_sc, g_sc, *, tb, pairs):
    L = WKV_CHUNK
    S = 2 * L
    width = pairs * LANES

    @pl.when(pl.program_id(2) == 0)
    def _():
        for ref in (st_ref, pr_ref, pk_ref, pv_ref, plo_ref,
                    q_sc, m_sc, yi_sc, u_sc, d_sc, bonus_sc, g_sc):
            ref[...] = jnp.zeros_like(ref)

    nc = tb // L
    units = [(c, p) for c in range(nc) for p in range(pairs)]
    chunks = range(len(units))

    st = [st_ref[p] for p in range(pairs)]
    ys = [[] for _ in range(pairs)]
    pending = list(range(nc))

    def phase_b(steps=1):
        for c in pending[:steps]:
            for p in range(pairs):
                u = c * pairs + p
                y_st = yi_sc[u] + _mm_nt(q_sc[u], st[p])
                ys[p].append(y_st[:L] + y_st[L:])
                st[p] = st[p] * d_sc[u] + _mm(st[p], m_sc[u]) + u_sc[u]
        del pending[:steps]

    phase_b(2)

    rf = _token_shift(r_ref, pr_ref, mur_ref)
    kf = _token_shift(k_ref, pk_ref, muk_ref)
    vf = _token_shift(v_ref, pv_ref, muv_ref)
    lof = _token_shift(lo_ref, plo_ref, mulo_ref)

    x1 = lof[:, :LANES]
    dec = w0_ref[...] + _mm(jnp.tanh(x1), wd_ref[...])
    z = -dec
    w = -(jnp.maximum(z, 0.0) + jnp.log(1.0 + jnp.exp(-jnp.abs(z)))) - 0.5
    logw = -jnp.exp(w)
    a = jax.nn.sigmoid(a0_ref[...] + _mm(x1, wa_ref[...]))
    g = _mm(jax.nn.sigmoid(lof[:, LANES:]), wg_ref[...])

    li = lax.broadcasted_iota(jnp.int32, (width, width), 0) // RWKV_HEAD
    lj = lax.broadcasted_iota(jnp.int32, (width, width), 1) // RWKV_HEAD
    head_ones = (li == lj).astype(BF16)

    def head_sum(x):
        return _mm_split(x, head_ones)

    kkr = kf * kk_ref[...]
    kkn = kkr / jnp.maximum(jnp.sqrt(head_sum(kkr * kkr)), 1e-12)
    k2 = kf * (1.0 + (a - 1.0) * ka_ref[...])
    bonus = head_sum(rf * k2 * rk_ref[...]) * vf
    avec = -kkn
    bvec = kkn * a

    ri = lax.broadcasted_iota(jnp.int32, (S, S), 0)
    ci = lax.broadcasted_iota(jnp.int32, (S, S), 1)
    same_head = (ri // L) == (ci // L)
    strict = same_head & (ri > ci)
    incl = same_head & (ri >= ci)
    same_blk = (ri // INV_BLOCK) == (ci // INV_BLOCK)
    lane_head = lax.broadcasted_iota(jnp.int32, (L, LANES), 1) // RWKV_HEAD
    ti = lax.broadcasted_iota(jnp.int32, (L, L), 0)
    tj = lax.broadcasted_iota(jnp.int32, (L, L), 1)
    tri = (ti >= tj).astype(BF16)

    def stack(x):
        return jnp.concatenate([jnp.where(lane_head == 0, x, 0.0),
                                jnp.where(lane_head == 1, x, 0.0)], axis=0)


    def part(x, c, p):
        return x[c * L:(c + 1) * L, p * LANES:(p + 1) * LANES]

    cum_c = [_mm_split_lhs(tri, logw[c * L:(c + 1) * L]) for c in range(nc)]
    a_s, r_s, b16, k16, v16, bh16, kh16, decay = [], [], [], [], [], [], [], []
    for c, p in units:
        lw = part(logw, c, p)
        cum = cum_c[c][:, p * LANES:(p + 1) * LANES]
        tot = cum[L - 1:L, :]
        e_neg = jnp.exp(-cum)
        e_tail = jnp.exp(tot - cum)
        bv, kv = part(bvec, c, p), part(k2, c, p)
        a_s.append(stack(part(avec, c, p) * jnp.exp(cum - lw)))
        r_s.append(stack(part(rf, c, p) * jnp.exp(cum)))
        b16.append(stack(bv * e_neg).astype(BF16))
        k16.append(stack(kv * e_neg).astype(BF16))
        v16.append(stack(part(vf, c, p)).astype(BF16))
        bh16.append(stack(bv * e_tail).astype(BF16))
        kh16.append(stack(kv * e_tail).astype(BF16))
        decay.append(jnp.exp(tot))
    a16 = [x.astype(BF16) for x in a_s]
    bigs = [_mm_nt(jnp.concatenate([a16[c], r_s[c].astype(BF16)], axis=0),
                   jnp.concatenate([b16[c], k16[c]], axis=0)) for c in chunks]
    n_ab = [jnp.where(strict, big[:S, :S], 0.0) for big in bigs]
    a_ak = [jnp.where(strict, big[:S, S:], 0.0) for big in bigs]
    a_br = [jnp.where(incl, big[S:, :S], 0.0) for big in bigs]
    a_kr = [jnp.where(incl, big[S:, S:], 0.0) for big in bigs]
    phase_b()

    nd = [jnp.where(same_blk, n, 0.0) for n in n_ab]
    ne = [jnp.where(same_blk, 0.0, n) for n in n_ab]
    akv = [_mm(a_ak[c], v16[c]) for c in chunks]
    pw = [_mm(x, x) for x in nd]
    sm = [nd[c] + pw[c] for c in chunks]
    phase_b()
    for _ in range(2):
        both = [_mm(pw[c], jnp.concatenate([sm[c], pw[c]], axis=1)) for c in chunks]
        sm = [sm[c] + both[c][:, :S] for c in chunks]
        pw = [both[c][:, S:] for c in chunks]
        phase_b()
    dm = [sm[c] + _mm(pw[c], sm[c]) for c in chunks]
    phase_b()

    krv = [_mm(a_kr[c], v16[c]) for c in chunks]
    vkh = [_mm_tn(v16[c], kh16[c]) for c in chunks]
    phase_b(len(pending))
    for p in range(pairs):
        st_ref[p] = st[p]

    y = jnp.concatenate([jnp.concatenate(yp, axis=0) for yp in ys], axis=1)
    mean = head_sum(y) * (1.0 / RWKV_HEAD)
    d = y - mean
    var = head_sum(d * d) * (1.0 / RWKV_HEAD)
    yn = d * lax.rsqrt(var + RWKV_GN_EPS) * gnw_ref[...] + gnb_ref[...]
    y_ref[0] = ((yn + bonus_sc[...]) * g_sc[...]).astype(y_ref.dtype)

    f1 = [ne[c] + _mm(dm[c], ne[c]) for c in chunks]
    f12 = [f1[c] + _mm(f1[c], f1[c]) for c in chunks]
    gm = [f1[c] + _mm(f1[c], f12[c]) for c in chunks]
    tm1 = [_ipm(gm[c], dm[c]) for c in chunks]
    zmat = [jnp.concatenate([a_s[c], akv[c]], axis=1) for c in chunks]
    xmat = [zmat[c] + _mm(tm1[c], zmat[c]) for c in chunks]
    x16 = [x.astype(BF16) for x in xmat]
    zx = [_mm(a_br[c], x16[c]) for c in chunks]
    mu_t = [_mm_tn(x16[c], bh16[c]) for c in chunks]
    q = [r_s[c] + zx[c][:, :LANES] for c in chunks]
    y_intra = [zx[c][:, LANES:] + krv[c] for c in chunks]
    m_t = [mu_t[c][:LANES] for c in chunks]
    u_t = [mu_t[c][LANES:] + vkh[c] for c in chunks]

    bonus_sc[...] = bonus
    g_sc[...] = g
    for u in chunks:
        q_sc[u] = q[u].astype(BF16)
        m_sc[u] = m_t[u].astype(BF16)
        yi_sc[u] = y_intra[u]
        u_sc[u] = u_t[u]
        d_sc[u] = decay[u]


def _mm_split_lhs(a, b):
    hi = b.astype(BF16)
    lo = (b - hi.astype(F32)).astype(BF16)
    return (jnp.dot(a, hi, preferred_element_type=F32)
            + jnp.dot(a, lo, preferred_element_type=F32))


def _wkv7(h_a, mu, w0, a0, k_k, k_a, r_k, gn_w, gn_b, wd_pad, wa_pad, wg, tb, pairs):
    b, t, _ = h_a.shape
    width = pairs * LANES
    groups = D_RWKV // width
    nt = t // tb
    units = (tb // WKV_CHUNK) * pairs
    col = lambda off: (lambda bi, p, ti: (bi, jnp.minimum(ti, nt - 1), off + p))
    vec = lambda off: (lambda bi, p, ti: (0, off + p))
    lora_blk = (3 * D_RWKV) // (2 * LANES)
    in_specs = [
        pl.BlockSpec((1, tb, width), col(0)),
        pl.BlockSpec((1, tb, width), col(groups)),
        pl.BlockSpec((1, tb, width), col(2 * groups)),
        pl.BlockSpec((1, tb, 2 * LANES),
                     lambda bi, p, ti: (bi, jnp.minimum(ti, nt - 1), lora_blk)),
        pl.BlockSpec((1, width), vec(0)),
        pl.BlockSpec((1, width), vec(groups)),
        pl.BlockSpec((1, width), vec(2 * groups)),
        pl.BlockSpec((1, 2 * LANES), lambda bi, p, ti: (0, lora_blk)),
    ] + [pl.BlockSpec((1, width), vec(0))] * 7 + [
        pl.BlockSpec((LANES, width), vec(0)),
        pl.BlockSpec((LANES, width), vec(0)),
        pl.BlockSpec((LANES, width), vec(0)),
    ]
    return pl.pallas_call(
        functools.partial(_wkv_kernel, tb=tb, pairs=pairs),
        grid=(b, groups, nt + 1),
        in_specs=in_specs,
        out_specs=pl.BlockSpec((1, tb, width),
                               lambda bi, p, ti: (bi, jnp.maximum(ti - 1, 0), p)),
        out_shape=jax.ShapeDtypeStruct((b, t, D_RWKV), BF16),
        scratch_shapes=[
            pltpu.VMEM((pairs, LANES, LANES), F32),
            pltpu.VMEM((1, width), F32),
            pltpu.VMEM((1, width), F32),
            pltpu.VMEM((1, width), F32),
            pltpu.VMEM((1, 2 * LANES), F32),
            pltpu.VMEM((units, 2 * WKV_CHUNK, LANES), BF16),
            pltpu.VMEM((units, LANES, LANES), BF16),
            pltpu.VMEM((units, 2 * WKV_CHUNK, LANES), F32),
            pltpu.VMEM((units, LANES, LANES), F32),
            pltpu.VMEM((units, 1, LANES), F32),
            pltpu.VMEM((tb, width), F32),
            pltpu.VMEM((tb, width), F32),
        ],
        compiler_params=pltpu.CompilerParams(
            dimension_semantics=("parallel", "parallel", "arbitrary"),
            vmem_limit_bytes=VMEM_LIMIT),
        name="wkv7",
    )(h_a, h_a, h_a, h_a, mu, mu, mu, mu, w0, a0, k_k, k_a, r_k, gn_w, gn_b,
      wd_pad, wa_pad, wg)


def _ret_kernel(q_ref, k_ref, v_ref, g_ref, cos_ref, sin_ref, gnw_ref, y_ref, state_ref):
    C = RET_CHUNK
    half = RET_HEAD // 2
    hd = pl.program_id(1)

    @pl.when(pl.program_id(2) == 0)
    def _():
        state_ref[...] = jnp.zeros_like(state_ref)

    log_gamma = jnp.float32(math.log(1.0 - 2.0 ** -5.0))
    for i in range(1, RET_HEADS):
        log_gamma = jnp.where(hd == i, jnp.float32(math.log(1.0 - 2.0 ** (-5.0 - i))), log_gamma)

    cos = cos_ref[...]
    sin = sin_ref[...]

    def rotary(t):
        t1, t2 = t[:, :half], t[:, half:]
        return jnp.concatenate([t1 * cos - t2 * sin, t1 * sin + t2 * cos], axis=1)

    q_all = rotary(q_ref[0])
    k_all = rotary(k_ref[0]) * (RET_HEAD ** -0.5)
    v_all = v_ref[0].astype(BF16)

    ri = lax.broadcasted_iota(jnp.int32, (C, C), 0)
    ci = lax.broadcasted_iota(jnp.int32, (C, C), 1)
    rel = (ri - ci).astype(F32)
    decay_mask = jnp.where(ri >= ci, jnp.exp(jnp.maximum(rel, 0.0) * log_gamma), 0.0)
    idx = lax.broadcasted_iota(jnp.int32, (C, 1), 0).astype(F32)
    q_decay = jnp.exp((idx + 1.0) * log_gamma)
    k_decay = jnp.exp((C - 1.0 - idx) * log_gamma)
    chunk_decay = jnp.exp(C * log_gamma)

    chunks = range(q_all.shape[0] // C)
    rows = lambda x, c: x[c * C:(c + 1) * C]
    q = [rows(q_all, c).astype(BF16) for c in chunks]
    k = [rows(k_all, c) for c in chunks]
    v = [rows(v_all, c) for c in chunks]
    scores = [_mm_nt(q[c], k[c]) * decay_mask for c in chunks]
    kv = [_mm_tn(k[c] * k_decay, v[c]) for c in chunks]
    intra = [_mm(scores[c], v[c]) for c in chunks]
    states = [state_ref[...]]
    for c in chunks:
        states.append(states[c] * chunk_decay + kv[c])
    state_ref[...] = states[-1]
    cross = [_mm(q[c], states[c]) * q_decay for c in chunks]

    y = jnp.concatenate([intra[c] + cross[c] for c in chunks], axis=0)
    mu = jnp.mean(y, axis=-1, keepdims=True)
    d = y - mu
    var = jnp.mean(d * d, axis=-1, keepdims=True)
    yn = d * lax.rsqrt(var + RET_GN_EPS) * gnw_ref[...]
    gate = g_ref[0]
    y_ref[0] = (yn * (gate * jax.nn.sigmoid(gate))).astype(y_ref.dtype)


def _retention(h_b, cos, sin, gn_w, tb):
    b, t, _ = h_b.shape
    col = lambda off: (lambda bi, h, n: (bi, n, off + h))
    return pl.pallas_call(
        _ret_kernel,
        grid=(b, RET_HEADS, t // tb),
        in_specs=[
            pl.BlockSpec((1, tb, RET_HEAD), col(0)),
            pl.BlockSpec((1, tb, RET_HEAD), col(RET_HEADS)),
            pl.BlockSpec((1, tb, RET_HEAD), col(2 * RET_HEADS)),
            pl.BlockSpec((1, tb, RET_HEAD), col(3 * RET_HEADS)),
            pl.BlockSpec((tb, RET_HEAD // 2), lambda bi, h, n: (n, 0)),
            pl.BlockSpec((tb, RET_HEAD // 2), lambda bi, h, n: (n, 0)),
            pl.BlockSpec((1, RET_HEAD), lambda bi, h, n: (0, h)),
        ],
        out_specs=pl.BlockSpec((1, tb, RET_HEAD), col(0)),
        out_shape=jax.ShapeDtypeStruct((b, t, D_RET), BF16),
        scratch_shapes=[pltpu.VMEM((RET_HEAD, RET_HEAD), F32)],
        compiler_params=pltpu.CompilerParams(
            dimension_semantics=("parallel", "parallel", "arbitrary")),
        name="retention",
    )(h_b, h_b, h_b, h_b, cos, sin, gn_w)


def _out_proj_kernel(x_ref, ya_ref, yb_ref, woa_ref, wob_ref, lnw_ref, lnb_ref, o_ref):
    mix = (jnp.dot(ya_ref[...], woa_ref[...], preferred_element_type=F32)
           + jnp.dot(yb_ref[...], wob_ref[...], preferred_element_type=F32))
    o_ref[...] = _layer_norm(ALPHA * x_ref[...] + mix, lnw_ref[...], lnb_ref[...])


def _out_proj(x2d, ya, yb, wo_a, wo_b, ln_w, ln_b, tm):
    m = x2d.shape[0]
    row = lambda w: pl.BlockSpec((tm, w), lambda i: (i, 0))
    full = lambda r, c: pl.BlockSpec((r, c), lambda i: (0, 0))
    return pl.pallas_call(
        _out_proj_kernel,
        grid=(m // tm,),
        in_specs=[row(D_MODEL), row(D_RWKV), row(D_RET), full(D_RWKV, D_MODEL),
                  full(D_RET, D_MODEL), full(1, D_MODEL), full(1, D_MODEL)],
        out_specs=row(D_MODEL),
        out_shape=jax.ShapeDtypeStruct((m, D_MODEL), F32),
        compiler_params=pltpu.CompilerParams(
            dimension_semantics=("parallel",), vmem_limit_bytes=VMEM_LIMIT),
        name="out_proj",
    )(x2d, ya, yb, wo_a, wo_b, ln_w, ln_b)


def _mlp_kernel(x_ref, wu_ref, wd_ref, lnw_ref, lnb_ref, o_ref):
    j = pl.program_id(1)

    @pl.when(j == 0)
    def _():
        o_ref[...] = ALPHA * x_ref[...]

    hid = jnp.dot(x_ref[...].astype(BF16), wu_ref[...], preferred_element_type=F32)
    hid = jnp.square(jnp.maximum(hid, 0.0))
    o_ref[...] += jnp.dot(hid.astype(BF16), wd_ref[...], preferred_element_type=F32)

    @pl.when(j == pl.num_programs(1) - 1)
    def _():
        o_ref[...] = _layer_norm(o_ref[...], lnw_ref[...], lnb_ref[...])


def _mlp(x1, w_up, w_down, ln_w, ln_b, tm, tf):
    m = x1.shape[0]
    return pl.pallas_call(
        _mlp_kernel,
        grid=(m // tm, D_FF // tf),
        in_specs=[
            pl.BlockSpec((tm, D_MODEL), lambda i, j: (i, 0)),
            pl.BlockSpec((D_MODEL, tf), lambda i, j: (0, j)),
            pl.BlockSpec((tf, D_MODEL), lambda i, j: (j, 0)),
            pl.BlockSpec((1, D_MODEL), lambda i, j: (0, 0)),
            pl.BlockSpec((1, D_MODEL), lambda i, j: (0, 0)),
        ],
        out_specs=pl.BlockSpec((tm, D_MODEL), lambda i, j: (i, 0)),
        out_shape=jax.ShapeDtypeStruct((m, D_MODEL), F32),
        compiler_params=pltpu.CompilerParams(
            dimension_semantics=("parallel", "arbitrary"), vmem_limit_bytes=VMEM_LIMIT),
        name="mlp",
    )(x1, w_up, w_down, ln_w, ln_b)


def _split_rotary_columns(w_ret):
    rows = w_ret.shape[0]
    qk = w_ret[:, :2 * D_RET].reshape(rows, 2 * RET_HEADS, RET_HEAD // 2, 2)
    qk = qk.transpose(0, 1, 3, 2).reshape(rows, 2 * D_RET)
    return jnp.concatenate([qk, w_ret[:, 2 * D_RET:]], axis=1)


def _layer(x, w_in, mu_shift, w0, w_lora_up, a0, a_lora_up, g_lora_up, k_k, k_a, r_k,
           rwkv_gn_w, rwkv_gn_b, ret_gn_w, w_o, ln1_w, ln1_b, w_up, w_down, ln2_w, ln2_b):
    b, t, _ = x.shape
    m = b * t
    x2d = x.reshape(m, D_MODEL)
    row = lambda p: p.reshape(1, -1)

    w_in16 = w_in.astype(BF16)
    w_a = w_in16[:, :RWKV_WIDTH]
    w_b = _split_rotary_columns(w_in16[:, RWKV_WIDTH:])
    h_a, h_b = _in_proj(x2d, w_a, w_b, tm=min(256, m))
    h_a = h_a.reshape(b, t, RWKV_WIDTH)
    h_b = h_b.reshape(b, t, RET_WIDTH)

    zeros = jnp.zeros((LORA_DECAY, D_RWKV), F32)
    wd_pad = jnp.concatenate([w_lora_up, zeros], axis=0).astype(BF16)
    wa_pad = jnp.concatenate([zeros, a_lora_up], axis=0).astype(BF16)
    y_a = _wkv7(h_a, row(mu_shift), row(w0), row(a0), row(k_k), row(k_a), row(r_k),
                row(rwkv_gn_w), row(rwkv_gn_b), wd_pad, wa_pad, g_lora_up.astype(BF16),
                tb=min(512, t), pairs=2)

    cos, sin = _rope_table(t, rows=min(512, t))
    y_b = _retention(h_b, cos, sin, row(ret_gn_w), tb=min(1024, t))

    w_o16 = w_o.astype(BF16)
    x1 = _out_proj(x2d, y_a.reshape(m, D_RWKV), y_b.reshape(m, D_RET), w_o16[:D_RWKV],
                   w_o16[D_RWKV:], row(ln1_w), row(ln1_b), tm=min(512, m))
    out = _mlp(x1, w_up.astype(BF16), w_down.astype(BF16), row(ln2_w), row(ln2_b),
               tm=min(512, m), tf=2048)
    return out.reshape(b, t, D_MODEL)


def kernel(x, w_in, mu_shift, w0, w_lora_up, a0, a_lora_up, g_lora_up, k_k, k_a, r_k,
           rwkv_gn_w, rwkv_gn_b, ret_gn_w, w_o, ln1_w, ln1_b, w_up, w_down, ln2_w, ln2_b):
    for l in range(DEPTH):
        x = _layer(x, w_in[l], mu_shift[l], w0[l], w_lora_up[l], a0[l], a_lora_up[l],
                   g_lora_up[l], k_k[l], k_a[l], r_k[l], rwkv_gn_w[l], rwkv_gn_b[l],
                   ret_gn_w[l], w_o[l], ln1_w[l], ln1_b[l], w_up[l], w_down[l],
                   ln2_w[l], ln2_b[l])
    return x
```

```python
import functools
import math

import jax
import jax.numpy as jnp
from jax import lax
from jax.experimental import pallas as pl
from jax.experimental.pallas import tpu as pltpu

F32 = jnp.float32
BF16 = jnp.bfloat16

D_MODEL = 2048
D_RWKV = 1024
D_RET = 1024
RWKV_HEAD = 64
RET_HEADS = 4
RET_HEAD = 256
RET_CHUNK = 128
LORA_DECAY = 64
LORA_A = 64
LORA_GATE = 128
D_FF = 4 * D_MODEL
ROPE_BASE = 10000.0
LN_EPS = 1e-5
RWKV_GN_EPS = 64e-5
RET_GN_EPS = 1e-6
DEPTH = 1
ALPHA = (2.0 * DEPTH) ** 0.25

RWKV_WIDTH = 3 * D_RWKV + LORA_DECAY + LORA_A + LORA_GATE
RET_WIDTH = 4 * D_RET
LANES = 128
HEAD_PAIRS = D_RWKV // LANES
WKV_CHUNK = 64
INV_BLOCK = 16
VMEM_LIMIT = 56 * 1024 * 1024


def _mm(a, b):
    return jnp.dot(a.astype(BF16), b.astype(BF16), preferred_element_type=F32)


def _mm_nt(a, b):
    return lax.dot_general(a.astype(BF16), b.astype(BF16), (((1,), (1,)), ((), ())),
                           preferred_element_type=F32)


def _mm_tn(a, b):
    return lax.dot_general(a.astype(BF16), b.astype(BF16), (((0,), (0,)), ((), ())),
                           preferred_element_type=F32)


def _layer_norm(z, w, b):
    mu = jnp.mean(z, axis=-1, keepdims=True)
    d = z - mu
    var = jnp.mean(d * d, axis=-1, keepdims=True)
    return d * lax.rsqrt(var + LN_EPS) * w + b


def _prep_w_in_kernel(w_ref, o_ref):
    blk = pl.program_id(0)
    first_qk = RWKV_WIDTH // RET_HEAD
    is_qk = (blk >= first_qk) & (blk < first_qk + 2 * RET_HEADS)
    src = lax.broadcasted_iota(jnp.int32, (RET_HEAD, RET_HEAD), 0)
    dst = lax.broadcasted_iota(jnp.int32, (RET_HEAD, RET_HEAD), 1)
    split = (src % 2) * (RET_HEAD // 2) + src // 2
    perm = (dst == jnp.where(is_qk, split, src)).astype(BF16)
    o_ref[...] = jnp.dot(w_ref[...].astype(BF16), perm,
                         preferred_element_type=F32).astype(BF16)


def _prep_w_in(w_in):
    k, n = w_in.shape
    return pl.pallas_call(
        _prep_w_in_kernel,
        grid=(n // RET_HEAD,),
        in_specs=[pl.BlockSpec((k, RET_HEAD), lambda j: (0, j))],
        out_specs=pl.BlockSpec((k, RET_HEAD), lambda j: (0, j)),
        out_shape=jax.ShapeDtypeStruct((k, n), BF16),
        name="prep_w_in",
    )(w_in)


def _in_proj_kernel(x_ref, w_ref, h_ref):
    h_ref[...] = jnp.dot(x_ref[...].astype(BF16), w_ref[...], preferred_element_type=F32)


def _in_proj(x2d, w16, tm):
    m = x2d.shape[0]
    n = w16.shape[1]
    return pl.pallas_call(
        _in_proj_kernel,
        grid=(m // tm,),
        in_specs=[
            pl.BlockSpec((tm, D_MODEL), lambda i: (i, 0)),
            pl.BlockSpec((D_MODEL, n), lambda i: (0, 0), pipeline_mode=pl.Buffered(1)),
        ],
        out_specs=pl.BlockSpec((tm, n), lambda i: (i, 0)),
        out_shape=jax.ShapeDtypeStruct((m, n), F32),
        compiler_params=pltpu.CompilerParams(
            dimension_semantics=("parallel",), vmem_limit_bytes=VMEM_LIMIT),
        name="in_proj",
    )(x2d, w16)


def _rope_kernel(cos_ref, sin_ref, *, rows):
    i = pl.program_id(0)
    half = RET_HEAD // 2
    lane = lax.broadcasted_iota(jnp.int32, (rows, half), 1).astype(F32)
    pos = (lax.broadcasted_iota(jnp.int32, (rows, half), 0) + i * rows).astype(F32)
    inv_freq = jnp.exp(lane * (-math.log(ROPE_BASE) / (half - 1)))
    theta = pos * inv_freq
    cos_ref[...] = jnp.cos(theta)
    sin_ref[...] = jnp.sin(theta)


def _rope_table(t, rows):
    half = RET_HEAD // 2
    return pl.pallas_call(
        functools.partial(_rope_kernel, rows=rows),
        grid=(t // rows,),
        out_specs=[pl.BlockSpec((rows, half), lambda i: (i, 0)),
                   pl.BlockSpec((rows, half), lambda i: (i, 0))],
        out_shape=[jax.ShapeDtypeStruct((t, half), F32),
                   jax.ShapeDtypeStruct((t, half), F32)],
        name="rope_table",
    )()


def _token_shift(x_ref, prev_ref, mu_ref):
    x = x_ref[0]
    rows = x.shape[0]
    rolled = pltpu.roll(x, 1, axis=0)
    row = lax.broadcasted_iota(jnp.int32, x.shape, 0)
    prev = jnp.where(row == 0, prev_ref[...], rolled)
    prev_ref[...] = x[rows - 1:rows, :]
    return x + (prev - x) * mu_ref[...]


def _ipm(a, b):
    return a + b + _mm(a, b)


def _wkv_kernel(r_ref, k_ref, v_ref, lo_ref, mur_ref, muk_ref, muv_ref, mulo_ref,
                w0_ref, a0_ref, kk_ref, ka_ref, rk_ref, gnw_ref, gnb_ref,
                wd_ref, wa_ref, wg_ref, y_ref,
                st_ref, pr_ref, pk_ref, pv_ref, plo_ref,
                q_sc, m_sc, yi_sc, u_sc, d_sc, bonus_sc, g_sc, *, tb, pairs):
    L = WKV_CHUNK
    S = 2 * L
    width = pairs * LANES

    @pl.when(pl.program_id(2) == 0)
    def _():
        for ref in (st_ref, pr_ref, pk_ref, pv_ref, plo_ref,
                    q_sc, m_sc, yi_sc, u_sc, d_sc, bonus_sc, g_sc):
            ref[...] = jnp.zeros_like(ref)

    nc = tb // L
    units = [(c, p) for c in range(nc) for p in range(pairs)]
    chunks = range(len(units))

    st = [st_ref[p] for p in range(pairs)]
    ys = [[] for _ in range(pairs)]
    pending = list(range(nc))

    def phase_b(steps=1):
        for c in pending[:steps]:
            for p in range(pairs):
                u = c * pairs + p
                y_st = yi_sc[u] + _mm_nt(q_sc[u], st[p])
                ys[p].append(y_st[:L] + y_st[L:])
                st[p] = st[p] * d_sc[u] + _mm(st[p], m_sc[u]) + u_sc[u]
        del pending[:steps]

    phase_b(2)

    rf = _token_shift(r_ref, pr_ref, mur_ref)
    kf = _token_shift(k_ref, pk_ref, muk_ref)
    vf = _token_shift(v_ref, pv_ref, muv_ref)
    lof = _token_shift(lo_ref, plo_ref, mulo_ref)

    x1 = lof[:, :LANES]
    dec = w0_ref[...] + _mm(jnp.tanh(x1), wd_ref[...])
    z = -dec
    w = -(jnp.maximum(z, 0.0) + jnp.log(1.0 + jnp.exp(-jnp.abs(z)))) - 0.5
    logw = -jnp.exp(w)
    a = jax.nn.sigmoid(a0_ref[...] + _mm(x1, wa_ref[...]))
    g = _mm(jax.nn.sigmoid(lof[:, LANES:]), wg_ref[...])

    li = lax.broadcasted_iota(jnp.int32, (width, width), 0) // RWKV_HEAD
    lj = lax.broadcasted_iota(jnp.int32, (width, width), 1) // RWKV_HEAD
    head_ones = (li == lj).astype(BF16)

    def head_sum(x):
        return _mm(x, head_ones)

    kkr = kf * kk_ref[...]
    kkn = kkr / jnp.maximum(jnp.sqrt(head_sum(kkr * kkr)), 1e-12)
    k2 = kf * (1.0 + (a - 1.0) * ka_ref[...])
    bonus = head_sum(rf * k2 * rk_ref[...]) * vf
    avec = -kkn
    bvec = kkn * a

    ri = lax.broadcasted_iota(jnp.int32, (S, S), 0)
    ci = lax.broadcasted_iota(jnp.int32, (S, S), 1)
    same_head = (ri // L) == (ci // L)
    strict = same_head & (ri > ci)
    incl = same_head & (ri >= ci)
    same_blk = (ri // INV_BLOCK) == (ci // INV_BLOCK)
    strict_blk = strict & same_blk
    strict_off = strict & jnp.logical_not(same_blk)
    lane_head = lax.broadcasted_iota(jnp.int32, (L, LANES), 1) // RWKV_HEAD
    ti = lax.broadcasted_iota(jnp.int32, (L, L), 0)
    tj = lax.broadcasted_iota(jnp.int32, (L, L), 1)
    tri = (ti >= tj).astype(BF16)

    def stack(x):
        return jnp.concatenate([jnp.where(lane_head == 0, x, 0.0),
                                jnp.where(lane_head == 1, x, 0.0)], axis=0)


    def part(x, c, p):
        return x[c * L:(c + 1) * L, p * LANES:(p + 1) * LANES]

    cum_c = [_mm_split_lhs(tri, logw[c * L:(c + 1) * L]) for c in range(nc)]
    a_s, r_s, b16, k16, v16, bh16, kh16, decay = [], [], [], [], [], [], [], []
    for c, p in units:
        lw = part(logw, c, p)
        cum = cum_c[c][:, p * LANES:(p + 1) * LANES]
        tot = cum[L - 1:L, :]
        e_neg = jnp.exp(-cum)
        e_tail = jnp.exp(tot - cum)
        bv, kv = part(bvec, c, p), part(k2, c, p)
        a_s.append(stack(part(avec, c, p) * jnp.exp(cum - lw)))
        r_s.append(stack(part(rf, c, p) * jnp.exp(cum)))
        b16.append(stack(bv * e_neg).astype(BF16))
        k16.append(stack(kv * e_neg).astype(BF16))
        v16.append(stack(part(vf, c, p)).astype(BF16))
        bh16.append(stack(bv * e_tail).astype(BF16))
        kh16.append(stack(kv * e_tail).astype(BF16))
        decay.append(jnp.exp(tot))
    a16 = [x.astype(BF16) for x in a_s]
    bigs = [_mm_nt(jnp.concatenate([a16[c], r_s[c].astype(BF16)], axis=0),
                   jnp.concatenate([b16[c], k16[c]], axis=0)) for c in chunks]
    nd = [jnp.where(strict_blk, big[:S, :S], 0.0) for big in bigs]
    ne = [jnp.where(strict_off, big[:S, :S], 0.0) for big in bigs]
    a_ak = [jnp.where(strict, big[:S, S:], 0.0).astype(BF16) for big in bigs]
    a_br = [jnp.where(incl, big[S:, :S], 0.0).astype(BF16) for big in bigs]
    a_kr = [jnp.where(incl, big[S:, S:], 0.0).astype(BF16) for big in bigs]
    phase_b()

    akv = [_mm(a_ak[c], v16[c]) for c in chunks]
    pw = [_mm(x, x) for x in nd]
    sm = [nd[c] + pw[c] for c in chunks]
    pw = [x.astype(BF16) for x in pw]
    phase_b()
    for _ in range(2):
        both = [_mm(pw[c], jnp.concatenate([sm[c].astype(BF16), pw[c]], axis=1)) for c in chunks]
        sm = [sm[c] + both[c][:, :S] for c in chunks]
        pw = [both[c][:, S:].astype(BF16) for c in chunks]
        phase_b()
    dm = [sm[c] + _mm(pw[c], sm[c]) for c in chunks]
    phase_b()

    krv = [_mm(a_kr[c], v16[c]) for c in chunks]
    vkh = [_mm_tn(v16[c], kh16[c]) for c in chunks]
    phase_b(len(pending))
    for p in range(pairs):
        st_ref[p] = st[p]

    y = jnp.concatenate([jnp.concatenate(yp, axis=0) for yp in ys], axis=1)
    mean = head_sum(y) * (1.0 / RWKV_HEAD)
    d = y - mean
    var = head_sum(d * d) * (1.0 / RWKV_HEAD)
    yn = d * lax.rsqrt(var + RWKV_GN_EPS) * gnw_ref[...] + gnb_ref[...]
    y_ref[0] = ((yn + bonus_sc[...]) * g_sc[...]).astype(y_ref.dtype)

    f1 = [ne[c] + _mm(dm[c], ne[c]) for c in chunks]
    f12 = [f1[c] + _mm(f1[c], f1[c]) for c in chunks]
    gm = [f1[c] + _mm(f1[c], f12[c]) for c in chunks]
    tm1 = [_ipm(gm[c], dm[c]) for c in chunks]
    zmat = [jnp.concatenate([a_s[c], akv[c]], axis=1) for c in chunks]
    xmat = [zmat[c] + _mm(tm1[c], zmat[c]) for c in chunks]
    x16 = [x.astype(BF16) for x in xmat]
    zx = [_mm(a_br[c], x16[c]) for c in chunks]
    mu_t = [_mm_tn(x16[c], bh16[c]) for c in chunks]
    q = [r_s[c] + zx[c][:, :LANES] for c in chunks]
    y_intra = [zx[c][:, LANES:] + krv[c] for c in chunks]
    m_t = [mu_t[c][:LANES] for c in chunks]
    u_t = [mu_t[c][LANES:] + vkh[c] for c in chunks]

    bonus_sc[...] = bonus
    g_sc[...] = g
    for u in chunks:
        q_sc[u] = q[u].astype(BF16)
        m_sc[u] = m_t[u].astype(BF16)
        yi_sc[u] = y_intra[u]
        u_sc[u] = u_t[u]
        d_sc[u] = decay[u]


def _mm_split_lhs(a, b):
    hi = b.astype(BF16)
    lo = (b - hi.astype(F32)).astype(BF16)
    return (jnp.dot(a, hi, preferred_element_type=F32)
            + jnp.dot(a, lo, preferred_element_type=F32))


def _wkv7(h_a, mu, w0, a0, k_k, k_a, r_k, gn_w, gn_b, wd_pad, wa_pad, wg, tb, pairs):
    b, t, _ = h_a.shape
    width = pairs * LANES
    groups = D_RWKV // width
    nt = t // tb
    units = (tb // WKV_CHUNK) * pairs
    col = lambda off: (lambda bi, p, ti: (bi, jnp.minimum(ti, nt - 1), off + p))
    vec = lambda off: (lambda bi, p, ti: (0, off + p))
    lora_blk = (3 * D_RWKV) // (2 * LANES)
    in_specs = [
        pl.BlockSpec((1, tb, width), col(0)),
        pl.BlockSpec((1, tb, width), col(groups)),
        pl.BlockSpec((1, tb, width), col(2 * groups)),
        pl.BlockSpec((1, tb, 2 * LANES),
                     lambda bi, p, ti: (bi, jnp.minimum(ti, nt - 1), lora_blk)),
        pl.BlockSpec((1, width), vec(0)),
        pl.BlockSpec((1, width), vec(groups)),
        pl.BlockSpec((1, width), vec(2 * groups)),
        pl.BlockSpec((1, 2 * LANES), lambda bi, p, ti: (0, lora_blk)),
    ] + [pl.BlockSpec((1, width), vec(0))] * 7 + [
        pl.BlockSpec((LANES, width), vec(0)),
        pl.BlockSpec((LANES, width), vec(0)),
        pl.BlockSpec((LANES, width), vec(0)),
    ]
    return pl.pallas_call(
        functools.partial(_wkv_kernel, tb=tb, pairs=pairs),
        grid=(b, groups, nt + 1),
        in_specs=in_specs,
        out_specs=pl.BlockSpec((1, tb, width),
                               lambda bi, p, ti: (bi, jnp.maximum(ti - 1, 0), p)),
        out_shape=jax.ShapeDtypeStruct((b, t, D_RWKV), BF16),
        scratch_shapes=[
            pltpu.VMEM((pairs, LANES, LANES), F32),
            pltpu.VMEM((1, width), F32),
            pltpu.VMEM((1, width), F32),
            pltpu.VMEM((1, width), F32),
            pltpu.VMEM((1, 2 * LANES), F32),
            pltpu.VMEM((units, 2 * WKV_CHUNK, LANES), BF16),
            pltpu.VMEM((units, LANES, LANES), BF16),
            pltpu.VMEM((units, 2 * WKV_CHUNK, LANES), F32),
            pltpu.VMEM((units, LANES, LANES), F32),
            pltpu.VMEM((units, 1, LANES), F32),
            pltpu.VMEM((tb, width), F32),
            pltpu.VMEM((tb, width), F32),
        ],
        compiler_params=pltpu.CompilerParams(
            dimension_semantics=("parallel", "parallel", "arbitrary"),
            vmem_limit_bytes=VMEM_LIMIT),
        name="wkv7",
    )(h_a, h_a, h_a, h_a, mu, mu, mu, mu, w0, a0, k_k, k_a, r_k, gn_w, gn_b,
      wd_pad, wa_pad, wg)


def _ret_kernel(q_ref, k_ref, v_ref, g_ref, cos_ref, sin_ref, gnw_ref, y_ref, state_ref):
    C = RET_CHUNK
    half = RET_HEAD // 2
    hd = pl.program_id(1)

    @pl.when(pl.program_id(2) == 0)
    def _():
        state_ref[...] = jnp.zeros_like(state_ref)

    log_gamma = jnp.float32(math.log(1.0 - 2.0 ** -5.0))
    for i in range(1, RET_HEADS):
        log_gamma = jnp.where(hd == i, jnp.float32(math.log(1.0 - 2.0 ** (-5.0 - i))), log_gamma)

    cos = cos_ref[...]
    sin = sin_ref[...]

    def rotary(t):
        t1, t2 = t[:, :half], t[:, half:]
        return jnp.concatenate([t1 * cos - t2 * sin, t1 * sin + t2 * cos], axis=1)

    q_all = rotary(q_ref[0])
    k_all = rotary(k_ref[0]) * (RET_HEAD ** -0.5)
    v_all = v_ref[0].astype(BF16)

    ri = lax.broadcasted_iota(jnp.int32, (C, C), 0)
    ci = lax.broadcasted_iota(jnp.int32, (C, C), 1)
    rel = (ri - ci).astype(F32)
    decay_mask = jnp.where(ri >= ci, jnp.exp(jnp.maximum(rel, 0.0) * log_gamma), 0.0)
    idx = lax.broadcasted_iota(jnp.int32, (C, 1), 0).astype(F32)
    q_decay = jnp.exp((idx + 1.0) * log_gamma)
    k_decay = jnp.exp((C - 1.0 - idx) * log_gamma)
    chunk_decay = jnp.exp(C * log_gamma)

    chunks = range(q_all.shape[0] // C)
    rows = lambda x, c: x[c * C:(c + 1) * C]
    q = [rows(q_all, c).astype(BF16) for c in chunks]
    k = [rows(k_all, c) for c in chunks]
    v = [rows(v_all, c) for c in chunks]
    scores = [_mm_nt(q[c], k[c]) * decay_mask for c in chunks]
    kv = [_mm_tn(k[c] * k_decay, v[c]) for c in chunks]
    intra = [_mm(scores[c], v[c]) for c in chunks]
    states = [state_ref[...]]
    for c in chunks:
        states.append(states[c] * chunk_decay + kv[c])
    state_ref[...] = states[-1]
    cross = [_mm(q[c], states[c]) * q_decay for c in chunks]

    y = jnp.concatenate([intra[c] + cross[c] for c in chunks], axis=0)
    mu = jnp.mean(y, axis=-1, keepdims=True)
    d = y - mu
    var = jnp.mean(d * d, axis=-1, keepdims=True)
    yn = d * lax.rsqrt(var + RET_GN_EPS) * gnw_ref[...]
    gate = g_ref[0]
    y_ref[0] = (yn * (gate * jax.nn.sigmoid(gate))).astype(y_ref.dtype)


def _retention(h_b, cos, sin, gn_w, tb):
    b, t, _ = h_b.shape
    first = RWKV_WIDTH // RET_HEAD
    col = lambda off: (lambda bi, h, n: (bi, n, first + off + h))
    return pl.pallas_call(
        _ret_kernel,
        grid=(b, RET_HEADS, t // tb),
        in_specs=[
            pl.BlockSpec((1, tb, RET_HEAD), col(0)),
            pl.BlockSpec((1, tb, RET_HEAD), col(RET_HEADS)),
            pl.BlockSpec((1, tb, RET_HEAD), col(2 * RET_HEADS)),
            pl.BlockSpec((1, tb, RET_HEAD), col(3 * RET_HEADS)),
            pl.BlockSpec((tb, RET_HEAD // 2), lambda bi, h, n: (n, 0)),
            pl.BlockSpec((tb, RET_HEAD // 2), lambda bi, h, n: (n, 0)),
            pl.BlockSpec((1, RET_HEAD), lambda bi, h, n: (0, h)),
        ],
        out_specs=pl.BlockSpec((1, tb, RET_HEAD), lambda bi, h, n: (bi, n, h)),
        out_shape=jax.ShapeDtypeStruct((b, t, D_RET), BF16),
        scratch_shapes=[pltpu.VMEM((RET_HEAD, RET_HEAD), F32)],
        compiler_params=pltpu.CompilerParams(
            dimension_semantics=("parallel", "parallel", "arbitrary")),
        name="retention",
    )(h_b, h_b, h_b, h_b, cos, sin, gn_w)


def _out_proj_kernel(x_ref, ya_ref, yb_ref, woa_ref, wob_ref, lnw_ref, lnb_ref, o_ref):
    mix = (jnp.dot(ya_ref[...], woa_ref[...], preferred_element_type=F32)
           + jnp.dot(yb_ref[...], wob_ref[...], preferred_element_type=F32))
    o_ref[...] = _layer_norm(ALPHA * x_ref[...] + mix, lnw_ref[...], lnb_ref[...])


def _out_proj(x2d, ya, yb, wo_a, wo_b, ln_w, ln_b, tm):
    m = x2d.shape[0]
    row = lambda w: pl.BlockSpec((tm, w), lambda i: (i, 0))
    full = lambda r, c: pl.BlockSpec((r, c), lambda i: (0, 0))
    return pl.pallas_call(
        _out_proj_kernel,
        grid=(m // tm,),
        in_specs=[row(D_MODEL), row(D_RWKV), row(D_RET), full(D_RWKV, D_MODEL),
                  full(D_RET, D_MODEL), full(1, D_MODEL), full(1, D_MODEL)],
        out_specs=row(D_MODEL),
        out_shape=jax.ShapeDtypeStruct((m, D_MODEL), F32),
        compiler_params=pltpu.CompilerParams(
            dimension_semantics=("parallel",), vmem_limit_bytes=VMEM_LIMIT),
        name="out_proj",
    )(x2d, ya, yb, wo_a, wo_b, ln_w, ln_b)


def _mlp_kernel(x_ref, wu_ref, wd_ref, lnw_ref, lnb_ref, o_ref):
    j = pl.program_id(1)

    @pl.when(j == 0)
    def _():
        o_ref[...] = ALPHA * x_ref[...]

    hid = jnp.dot(x_ref[...].astype(BF16), wu_ref[...], preferred_element_type=F32)
    hid = jnp.square(jnp.maximum(hid, 0.0))
    o_ref[...] += jnp.dot(hid.astype(BF16), wd_ref[...], preferred_element_type=F32)

    @pl.when(j == pl.num_programs(1) - 1)
    def _():
        o_ref[...] = _layer_norm(o_ref[...], lnw_ref[...], lnb_ref[...])


def _mlp(x1, w_up, w_down, ln_w, ln_b, tm, tf):
    m = x1.shape[0]
    return pl.pallas_call(
        _mlp_kernel,
        grid=(m // tm, D_FF // tf),
        in_specs=[
            pl.BlockSpec((tm, D_MODEL), lambda i, j: (i, 0)),
            pl.BlockSpec((D_MODEL, tf), lambda i, j: (0, j)),
            pl.BlockSpec((tf, D_MODEL), lambda i, j: (j, 0)),
            pl.BlockSpec((1, D_MODEL), lambda i, j: (0, 0)),
            pl.BlockSpec((1, D_MODEL), lambda i, j: (0, 0)),
        ],
        out_specs=pl.BlockSpec((tm, D_MODEL), lambda i, j: (i, 0)),
        out_shape=jax.ShapeDtypeStruct((m, D_MODEL), F32),
        compiler_params=pltpu.CompilerParams(
            dimension_semantics=("parallel", "arbitrary"), vmem_limit_bytes=VMEM_LIMIT),
        name="mlp",
    )(x1, w_up, w_down, ln_w, ln_b)


def _layer(x, w_in, mu_shift, w0, w_lora_up, a0, a_lora_up, g_lora_up, k_k, k_a, r_k,
           rwkv_gn_w, rwkv_gn_b, ret_gn_w, w_o, ln1_w, ln1_b, w_up, w_down, ln2_w, ln2_b):
    b, t, _ = x.shape
    m = b * t
    x2d = x.reshape(m, D_MODEL)
    row = lambda p: p.reshape(1, -1)

    h = _in_proj(x2d, _prep_w_in(w_in), tm=min(256, m)).reshape(b, t, RWKV_WIDTH + RET_WIDTH)

    zeros = jnp.zeros((LORA_DECAY, D_RWKV), F32)
    wd_pad = jnp.concatenate([w_lora_up, zeros], axis=0).astype(BF16)
    wa_pad = jnp.concatenate([zeros, a_lora_up], axis=0).astype(BF16)
    y_a = _wkv7(h, row(mu_shift), row(w0), row(a0), row(k_k), row(k_a), row(r_k),
                row(rwkv_gn_w), row(rwkv_gn_b), wd_pad, wa_pad, g_lora_up.astype(BF16),
                tb=min(512, t), pairs=2)

    cos, sin = _rope_table(t, rows=min(512, t))
    y_b = _retention(h, cos, sin, row(ret_gn_w), tb=min(1024, t))

    w_o16 = w_o.astype(BF16)
    x1 = _out_proj(x2d, y_a.reshape(m, D_RWKV), y_b.reshape(m, D_RET), w_o16[:D_RWKV],
                   w_o16[D_RWKV:], row(ln1_w), row(ln1_b), tm=min(512, m))
    out = _mlp(x1, w_up.astype(BF16), w_down.astype(BF16), row(ln2_w), row(ln2_b),
               tm=min(512, m), tf=2048)
    return out.reshape(b, t, D_MODEL)


def kernel(x, w_in, mu_shift, w0, w_lora_up, a0, a_lora_up, g_lora_up, k_k, k_a, r_k,
           rwkv_gn_w, rwkv_gn_b, ret_gn_w, w_o, ln1_w, ln1_b, w_up, w_down, ln2_w, ln2_b):
    for l in range(DEPTH):
        x = _layer(x, w_in[l], mu_shift[l], w0[l], w_lora_up[l], a0[l], a_lora_up[l],
                   g_lora_up[l], k_k[l], k_a[l], r_k[l], rwkv_gn_w[l], rwkv_gn_b[l],
                   ret_gn_w[l], w_o[l], ln1_w[l], ln1_b[l], w_up[l], w_down[l],
                   ln2_w[l], ln2_b[l])
    return x
```

```python
import functools
import math

import jax
import jax.numpy as jnp
from jax import lax
from jax.experimental import pallas as pl
from jax.experimental.pallas import tpu as pltpu

F32 = jnp.float32
BF16 = jnp.bfloat16

D_MODEL = 2048
D_RWKV = 1024
D_RET = 1024
RWKV_HEAD = 64
RET_HEADS = 4
RET_HEAD = 256
RET_CHUNK = 128
LORA_DECAY = 64
LORA_A = 64
LORA_GATE = 128
D_FF = 4 * D_MODEL
ROPE_BASE = 10000.0
LN_EPS = 1e-5
RWKV_GN_EPS = 64e-5
RET_GN_EPS = 1e-6
DEPTH = 1
ALPHA = (2.0 * DEPTH) ** 0.25

RWKV_WIDTH = 3 * D_RWKV + LORA_DECAY + LORA_A + LORA_GATE
RET_WIDTH = 4 * D_RET
LANES = 128
WKV_CHUNK = 64
INV_BLOCK = 16
VMEM_LIMIT = 56 * 1024 * 1024

IN_PROJ_ROWS = 256
WKV_ROWS = 512
WKV_PAIRS = 2
RET_ROWS = 1024
ROPE_ROWS = 512
OUT_PROJ_ROWS = 512
MLP_ROWS = 512
MLP_COLS = 2048


def _mm(a, b):
    return jnp.dot(a.astype(BF16), b.astype(BF16), preferred_element_type=F32)


def _mm_nt(a, b):
    return lax.dot_general(a.astype(BF16), b.astype(BF16), (((1,), (1,)), ((), ())),
                           preferred_element_type=F32)


def _mm_tn(a, b):
    return lax.dot_general(a.astype(BF16), b.astype(BF16), (((0,), (0,)), ((), ())),
                           preferred_element_type=F32)


def _layer_norm(z, w, b):
    mu = jnp.mean(z, axis=-1, keepdims=True)
    d = z - mu
    var = jnp.mean(d * d, axis=-1, keepdims=True)
    return d * lax.rsqrt(var + LN_EPS) * w + b


def _prep_w_in_kernel(w_ref, o_ref):
    blk = pl.program_id(0)
    first_qk = RWKV_WIDTH // RET_HEAD
    is_qk = (blk >= first_qk) & (blk < first_qk + 2 * RET_HEADS)
    src = lax.broadcasted_iota(jnp.int32, (RET_HEAD, RET_HEAD), 0)
    dst = lax.broadcasted_iota(jnp.int32, (RET_HEAD, RET_HEAD), 1)
    split = (src % 2) * (RET_HEAD // 2) + src // 2
    perm = (dst == jnp.where(is_qk, split, src)).astype(BF16)
    o_ref[...] = jnp.dot(w_ref[...].astype(BF16), perm,
                         preferred_element_type=F32).astype(BF16)


def _prep_w_in(w_in):
    k, n = w_in.shape
    return pl.pallas_call(
        _prep_w_in_kernel,
        grid=(n // RET_HEAD,),
        in_specs=[pl.BlockSpec((k, RET_HEAD), lambda j: (0, j))],
        out_specs=pl.BlockSpec((k, RET_HEAD), lambda j: (0, j)),
        out_shape=jax.ShapeDtypeStruct((k, n), BF16),
        name="prep_w_in",
    )(w_in)


def _in_proj_kernel(x_ref, w_ref, h_ref):
    h_ref[...] = jnp.dot(x_ref[...].astype(BF16), w_ref[...], preferred_element_type=F32)


def _in_proj(x2d, w16, tm):
    m = x2d.shape[0]
    n = w16.shape[1]
    return pl.pallas_call(
        _in_proj_kernel,
        grid=(m // tm,),
        in_specs=[
            pl.BlockSpec((tm, D_MODEL), lambda i: (i, 0)),
            pl.BlockSpec((D_MODEL, n), lambda i: (0, 0), pipeline_mode=pl.Buffered(1)),
        ],
        out_specs=pl.BlockSpec((tm, n), lambda i: (i, 0)),
        out_shape=jax.ShapeDtypeStruct((m, n), F32),
        compiler_params=pltpu.CompilerParams(
            dimension_semantics=("parallel",), vmem_limit_bytes=VMEM_LIMIT),
        name="in_proj",
    )(x2d, w16)


def _rope_kernel(cos_ref, sin_ref, *, rows):
    i = pl.program_id(0)
    half = RET_HEAD // 2
    lane = lax.broadcasted_iota(jnp.int32, (rows, half), 1).astype(F32)
    pos = (lax.broadcasted_iota(jnp.int32, (rows, half), 0) + i * rows).astype(F32)
    inv_freq = jnp.exp(lane * (-math.log(ROPE_BASE) / (half - 1)))
    theta = pos * inv_freq
    cos_ref[...] = jnp.cos(theta)
    sin_ref[...] = jnp.sin(theta)


def _rope_table(t, rows):
    half = RET_HEAD // 2
    return pl.pallas_call(
        functools.partial(_rope_kernel, rows=rows),
        grid=(t // rows,),
        out_specs=[pl.BlockSpec((rows, half), lambda i: (i, 0)),
                   pl.BlockSpec((rows, half), lambda i: (i, 0))],
        out_shape=[jax.ShapeDtypeStruct((t, half), F32),
                   jax.ShapeDtypeStruct((t, half), F32)],
        name="rope_table",
    )()


def _token_shift(x_ref, prev_ref, mu_ref):
    x = x_ref[0]
    rows = x.shape[0]
    rolled = pltpu.roll(x, 1, axis=0)
    row = lax.broadcasted_iota(jnp.int32, x.shape, 0)
    prev = jnp.where(row == 0, prev_ref[...], rolled)
    prev_ref[...] = x[rows - 1:rows, :]
    return x + (prev - x) * mu_ref[...]


def _ipm(a, b):
    return a + b + _mm(a, b)


def _wkv_kernel(r_ref, k_ref, v_ref, lo_ref, mur_ref, muk_ref, muv_ref, mulo_ref,
                w0_ref, a0_ref, kk_ref, ka_ref, rk_ref, gnw_ref, gnb_ref,
                wd_ref, wa_ref, wg_ref, y_ref,
                st_ref, pr_ref, pk_ref, pv_ref, plo_ref,
                q_sc, m_sc, yi_sc, u_sc, d_sc, bonus_sc, g_sc, *, tb, pairs):
    L = WKV_CHUNK
    S = 2 * L
    width = pairs * LANES

    @pl.when(pl.program_id(2) == 0)
    def _():
        for ref in (st_ref, pr_ref, pk_ref, pv_ref, plo_ref,
                    q_sc, m_sc, yi_sc, u_sc, d_sc, bonus_sc, g_sc):
            ref[...] = jnp.zeros_like(ref)

    nc = tb // L
    units = [(c, p) for c in range(nc) for p in range(pairs)]
    chunks = range(len(units))

    st = [st_ref[p] for p in range(pairs)]
    ys = [[] for _ in range(pairs)]
    pending = list(range(nc))

    def phase_b(steps=1):
        for c in pending[:steps]:
            for p in range(pairs):
                u = c * pairs + p
                y_st = yi_sc[u] + _mm_nt(q_sc[u], st[p])
                ys[p].append(y_st[:L] + y_st[L:])
                st[p] = st[p] * d_sc[u] + _mm(st[p], m_sc[u]) + u_sc[u]
        del pending[:steps]

    phase_b(2)

    rf = _token_shift(r_ref, pr_ref, mur_ref)
    kf = _token_shift(k_ref, pk_ref, muk_ref)
    vf = _token_shift(v_ref, pv_ref, muv_ref)
    lof = _token_shift(lo_ref, plo_ref, mulo_ref)

    x1 = lof[:, :LANES]
    dec = w0_ref[...] + _mm(jnp.tanh(x1), wd_ref[...])
    z = -dec
    w = -(jnp.maximum(z, 0.0) + jnp.log(1.0 + jnp.exp(-jnp.abs(z)))) - 0.5
    logw = -jnp.exp(w)
    a = jax.nn.sigmoid(a0_ref[...] + _mm(x1, wa_ref[...]))
    g = _mm(jax.nn.sigmoid(lof[:, LANES:]), wg_ref[...])

    li = lax.broadcasted_iota(jnp.int32, (width, width), 0) // RWKV_HEAD
    lj = lax.broadcasted_iota(jnp.int32, (width, width), 1) // RWKV_HEAD
    head_ones = (li == lj).astype(BF16)

    def head_sum(x):
        return _mm(x, head_ones)

    kkr = kf * kk_ref[...]
    kkn = kkr / jnp.maximum(jnp.sqrt(head_sum(kkr * kkr)), 1e-12)
    k2 = kf * (1.0 + (a - 1.0) * ka_ref[...])
    bonus = head_sum(rf * k2 * rk_ref[...]) * vf
    avec = -kkn
    bvec = kkn * a

    ri = lax.broadcasted_iota(jnp.int32, (S, S), 0)
    ci = lax.broadcasted_iota(jnp.int32, (S, S), 1)
    same_head = (ri // L) == (ci // L)
    strict = same_head & (ri > ci)
    incl = same_head & (ri >= ci)
    same_blk = (ri // INV_BLOCK) == (ci // INV_BLOCK)
    strict_blk = strict & same_blk
    strict_off = strict & jnp.logical_not(same_blk)
    lane_head = lax.broadcasted_iota(jnp.int32, (L, LANES), 1) // RWKV_HEAD
    ti = lax.broadcasted_iota(jnp.int32, (L, L), 0)
    tj = lax.broadcasted_iota(jnp.int32, (L, L), 1)
    tri = (ti >= tj).astype(BF16)

    def stack(x):
        return jnp.concatenate([jnp.where(lane_head == 0, x, 0.0),
                                jnp.where(lane_head == 1, x, 0.0)], axis=0)

    def twice(x):
        return jnp.concatenate([x, x], axis=0)


    def part(x, c, p):
        return x[c * L:(c + 1) * L, p * LANES:(p + 1) * LANES]

    cum_c = [_mm_split_lhs(tri, logw[c * L:(c + 1) * L]) for c in range(nc)]
    a_s, r_s, b16, k16, v16, bh16, kh16, decay = [], [], [], [], [], [], [], []
    for c, p in units:
        lw = part(logw, c, p)
        cum = cum_c[c][:, p * LANES:(p + 1) * LANES]
        tot = cum[L - 1:L, :]
        e_neg = jnp.exp(-cum)
        e_tail = jnp.exp(tot - cum)
        bv, kv = part(bvec, c, p), part(k2, c, p)
        a_s.append(stack(part(avec, c, p) * jnp.exp(cum - lw)))
        r_s.append(stack(part(rf, c, p) * jnp.exp(cum)))
        b16.append(twice((bv * e_neg).astype(BF16)))
        k16.append(twice((kv * e_neg).astype(BF16)))
        v16.append(stack(part(vf, c, p)).astype(BF16))
        bh16.append(stack(bv * e_tail).astype(BF16))
        kh16.append(stack(kv * e_tail).astype(BF16))
        decay.append(jnp.exp(tot))
    a16 = [x.astype(BF16) for x in a_s]
    bigs = [_mm_nt(jnp.concatenate([a16[c], r_s[c].astype(BF16)], axis=0),
                   jnp.concatenate([b16[c], k16[c]], axis=0)) for c in chunks]
    nd = [jnp.where(strict_blk, big[:S, :S], 0.0) for big in bigs]
    ne = [jnp.where(strict_off, big[:S, :S], 0.0) for big in bigs]
    a_ak = [jnp.where(strict, big[:S, S:], 0.0).astype(BF16) for big in bigs]
    a_br = [jnp.where(incl, big[S:, :S], 0.0).astype(BF16) for big in bigs]
    a_kr = [jnp.where(incl, big[S:, S:], 0.0).astype(BF16) for big in bigs]
    phase_b()

    akv = [_mm(a_ak[c], v16[c]) for c in chunks]
    pw = [_mm(x, x) for x in nd]
    sm = [nd[c] + pw[c] for c in chunks]
    pw = [x.astype(BF16) for x in pw]
    phase_b()
    for _ in range(2):
        both = [_mm(pw[c], jnp.concatenate([sm[c].astype(BF16), pw[c]], axis=1)) for c in chunks]
        sm = [sm[c] + both[c][:, :S] for c in chunks]
        pw = [both[c][:, S:].astype(BF16) for c in chunks]
        phase_b()
    dm = [sm[c] + _mm(pw[c], sm[c]) for c in chunks]
    phase_b()

    krv = [_mm(a_kr[c], v16[c]) for c in chunks]
    vkh = [_mm_tn(v16[c], kh16[c]) for c in chunks]
    phase_b(len(pending))
    for p in range(pairs):
        st_ref[p] = st[p]

    y = jnp.concatenate([jnp.concatenate(yp, axis=0) for yp in ys], axis=1)
    mean = head_sum(y) * (1.0 / RWKV_HEAD)
    d = y - mean
    var = head_sum(d * d) * (1.0 / RWKV_HEAD)
    yn = d * lax.rsqrt(var + RWKV_GN_EPS) * gnw_ref[...] + gnb_ref[...]
    y_ref[0] = ((yn + bonus_sc[...]) * g_sc[...]).astype(y_ref.dtype)

    f1 = [ne[c] + _mm(dm[c], ne[c]) for c in chunks]
    f12 = [f1[c] + _mm(f1[c], f1[c]) for c in chunks]
    gm = [f1[c] + _mm(f1[c], f12[c]) for c in chunks]
    tm1 = [_ipm(gm[c], dm[c]) for c in chunks]
    zmat = [jnp.concatenate([a_s[c], akv[c]], axis=1) for c in chunks]
    xmat = [zmat[c] + _mm(tm1[c], zmat[c]) for c in chunks]
    x16 = [x.astype(BF16) for x in xmat]
    zx = [_mm(a_br[c], x16[c]) for c in chunks]
    mu_t = [_mm_tn(x16[c], bh16[c]) for c in chunks]
    q = [r_s[c] + zx[c][:, :LANES] for c in chunks]
    y_intra = [zx[c][:, LANES:] + krv[c] for c in chunks]
    m_t = [mu_t[c][:LANES] for c in chunks]
    u_t = [mu_t[c][LANES:] + vkh[c] for c in chunks]

    bonus_sc[...] = bonus
    g_sc[...] = g
    for u in chunks:
        q_sc[u] = q[u].astype(BF16)
        m_sc[u] = m_t[u].astype(BF16)
        yi_sc[u] = y_intra[u]
        u_sc[u] = u_t[u]
        d_sc[u] = decay[u]


def _mm_split_lhs(a, b):
    hi = b.astype(BF16)
    lo = (b - hi.astype(F32)).astype(BF16)
    return (jnp.dot(a, hi, preferred_element_type=F32)
            + jnp.dot(a, lo, preferred_element_type=F32))


def _wkv7(h_a, mu, w0, a0, k_k, k_a, r_k, gn_w, gn_b, wd_pad, wa_pad, wg, tb, pairs):
    b, t, _ = h_a.shape
    width = pairs * LANES
    groups = D_RWKV // width
    nt = t // tb
    units = (tb // WKV_CHUNK) * pairs
    col = lambda off: (lambda bi, p, ti: (bi, jnp.minimum(ti, nt - 1), off + p))
    vec = lambda off: (lambda bi, p, ti: (0, off + p))
    lora_blk = (3 * D_RWKV) // (2 * LANES)
    in_specs = [
        pl.BlockSpec((1, tb, width), col(0)),
        pl.BlockSpec((1, tb, width), col(groups)),
        pl.BlockSpec((1, tb, width), col(2 * groups)),
        pl.BlockSpec((1, tb, 2 * LANES),
                     lambda bi, p, ti: (bi, jnp.minimum(ti, nt - 1), lora_blk)),
        pl.BlockSpec((1, width), vec(0)),
        pl.BlockSpec((1, width), vec(groups)),
        pl.BlockSpec((1, width), vec(2 * groups)),
        pl.BlockSpec((1, 2 * LANES), lambda bi, p, ti: (0, lora_blk)),
    ] + [pl.BlockSpec((1, width), vec(0))] * 7 + [
        pl.BlockSpec((LANES, width), vec(0)),
        pl.BlockSpec((LANES, width), vec(0)),
        pl.BlockSpec((LANES, width), vec(0)),
    ]
    return pl.pallas_call(
        functools.partial(_wkv_kernel, tb=tb, pairs=pairs),
        grid=(b, groups, nt + 1),
        in_specs=in_specs,
        out_specs=pl.BlockSpec((1, tb, width),
                               lambda bi, p, ti: (bi, jnp.maximum(ti - 1, 0), p)),
        out_shape=jax.ShapeDtypeStruct((b, t, D_RWKV), BF16),
        scratch_shapes=[
            pltpu.VMEM((pairs, LANES, LANES), F32),
            pltpu.VMEM((1, width), F32),
            pltpu.VMEM((1, width), F32),
            pltpu.VMEM((1, width), F32),
            pltpu.VMEM((1, 2 * LANES), F32),
            pltpu.VMEM((units, 2 * WKV_CHUNK, LANES), BF16),
            pltpu.VMEM((units, LANES, LANES), BF16),
            pltpu.VMEM((units, 2 * WKV_CHUNK, LANES), F32),
            pltpu.VMEM((units, LANES, LANES), F32),
            pltpu.VMEM((units, 1, LANES), F32),
            pltpu.VMEM((tb, width), F32),
            pltpu.VMEM((tb, width), F32),
        ],
        compiler_params=pltpu.CompilerParams(
            dimension_semantics=("parallel", "parallel", "arbitrary"),
            vmem_limit_bytes=VMEM_LIMIT),
        name="wkv7",
    )(h_a, h_a, h_a, h_a, mu, mu, mu, mu, w0, a0, k_k, k_a, r_k, gn_w, gn_b,
      wd_pad, wa_pad, wg)


def _ret_kernel(q_ref, k_ref, v_ref, g_ref, cos_ref, sin_ref, gnw_ref, y_ref, state_ref):
    C = RET_CHUNK
    half = RET_HEAD // 2
    hd = pl.program_id(1)

    @pl.when(pl.program_id(2) == 0)
    def _():
        state_ref[...] = jnp.zeros_like(state_ref)

    log_gamma = jnp.float32(math.log(1.0 - 2.0 ** -5.0))
    for i in range(1, RET_HEADS):
        log_gamma = jnp.where(hd == i, jnp.float32(math.log(1.0 - 2.0 ** (-5.0 - i))), log_gamma)

    cos = cos_ref[...]
    sin = sin_ref[...]

    def rotary(t):
        t1, t2 = t[:, :half], t[:, half:]
        return jnp.concatenate([t1 * cos - t2 * sin, t1 * sin + t2 * cos], axis=1)

    q_all = rotary(q_ref[0])
    k_all = rotary(k_ref[0]) * (RET_HEAD ** -0.5)
    v_all = v_ref[0].astype(BF16)

    ri = lax.broadcasted_iota(jnp.int32, (C, C), 0)
    ci = lax.broadcasted_iota(jnp.int32, (C, C), 1)
    rel = (ri - ci).astype(F32)
    decay_mask = jnp.where(ri >= ci, jnp.exp(jnp.maximum(rel, 0.0) * log_gamma), 0.0)
    idx = lax.broadcasted_iota(jnp.int32, (C, 1), 0).astype(F32)
    q_decay = jnp.exp((idx + 1.0) * log_gamma)
    k_decay = jnp.exp((C - 1.0 - idx) * log_gamma)
    chunk_decay = jnp.exp(C * log_gamma)

    chunks = range(q_all.shape[0] // C)
    rows = lambda x, c: x[c * C:(c + 1) * C]
    q = [rows(q_all, c).astype(BF16) for c in chunks]
    k = [rows(k_all, c) for c in chunks]
    v = [rows(v_all, c) for c in chunks]
    scores = [_mm_nt(q[c], k[c]) * decay_mask for c in chunks]
    kv = [_mm_tn(k[c] * k_decay, v[c]) for c in chunks]
    intra = [_mm(scores[c], v[c]) for c in chunks]
    states = [state_ref[...]]
    for c in chunks:
        states.append(states[c] * chunk_decay + kv[c])
    state_ref[...] = states[-1]
    cross = [_mm(q[c], states[c]) * q_decay for c in chunks]

    y = jnp.concatenate([intra[c] + cross[c] for c in chunks], axis=0)
    mu = jnp.mean(y, axis=-1, keepdims=True)
    d = y - mu
    var = jnp.mean(d * d, axis=-1, keepdims=True)
    yn = d * lax.rsqrt(var + RET_GN_EPS) * gnw_ref[...]
    gate = g_ref[0]
    y_ref[0] = (yn * (gate * jax.nn.sigmoid(gate))).astype(y_ref.dtype)


def _retention(h_b, cos, sin, gn_w, tb):
    b, t, _ = h_b.shape
    first = RWKV_WIDTH // RET_HEAD
    col = lambda off: (lambda bi, h, n: (bi, n, first + off + h))
    return pl.pallas_call(
        _ret_kernel,
        grid=(b, RET_HEADS, t // tb),
        in_specs=[
            pl.BlockSpec((1, tb, RET_HEAD), col(0)),
            pl.BlockSpec((1, tb, RET_HEAD), col(RET_HEADS)),
            pl.BlockSpec((1, tb, RET_HEAD), col(2 * RET_HEADS)),
            pl.BlockSpec((1, tb, RET_HEAD), col(3 * RET_HEADS)),
            pl.BlockSpec((tb, RET_HEAD // 2), lambda bi, h, n: (n, 0)),
            pl.BlockSpec((tb, RET_HEAD // 2), lambda bi, h, n: (n, 0)),
            pl.BlockSpec((1, RET_HEAD), lambda bi, h, n: (0, h)),
        ],
        out_specs=pl.BlockSpec((1, tb, RET_HEAD), lambda bi, h, n: (bi, n, h)),
        out_shape=jax.ShapeDtypeStruct((b, t, D_RET), BF16),
        scratch_shapes=[pltpu.VMEM((RET_HEAD, RET_HEAD), F32)],
        compiler_params=pltpu.CompilerParams(
            dimension_semantics=("parallel", "parallel", "arbitrary")),
        name="retention",
    )(h_b, h_b, h_b, h_b, cos, sin, gn_w)


def _out_proj_kernel(x_ref, ya_ref, yb_ref, woa_ref, wob_ref, lnw_ref, lnb_ref, o_ref):
    mix = (jnp.dot(ya_ref[...], woa_ref[...], preferred_element_type=F32)
           + jnp.dot(yb_ref[...], wob_ref[...], preferred_element_type=F32))
    o_ref[...] = _layer_norm(ALPHA * x_ref[...] + mix, lnw_ref[...], lnb_ref[...])


def _out_proj(x2d, ya, yb, wo_a, wo_b, ln_w, ln_b, tm):
    m = x2d.shape[0]
    row = lambda w: pl.BlockSpec((tm, w), lambda i: (i, 0))
    full = lambda r, c: pl.BlockSpec((r, c), lambda i: (0, 0))
    return pl.pallas_call(
        _out_proj_kernel,
        grid=(m // tm,),
        in_specs=[row(D_MODEL), row(D_RWKV), row(D_RET), full(D_RWKV, D_MODEL),
                  full(D_RET, D_MODEL), full(1, D_MODEL), full(1, D_MODEL)],
        out_specs=row(D_MODEL),
        out_shape=jax.ShapeDtypeStruct((m, D_MODEL), F32),
        compiler_params=pltpu.CompilerParams(
            dimension_semantics=("parallel",), vmem_limit_bytes=VMEM_LIMIT),
        name="out_proj",
    )(x2d, ya, yb, wo_a, wo_b, ln_w, ln_b)


def _mlp_kernel(x_ref, wu_ref, wd_ref, lnw_ref, lnb_ref, o_ref):
    j = pl.program_id(1)

    @pl.when(j == 0)
    def _():
        o_ref[...] = ALPHA * x_ref[...]

    hid = jnp.dot(x_ref[...].astype(BF16), wu_ref[...], preferred_element_type=F32)
    hid = jnp.square(jnp.maximum(hid, 0.0))
    o_ref[...] += jnp.dot(hid.astype(BF16), wd_ref[...], preferred_element_type=F32)

    @pl.when(j == pl.num_programs(1) - 1)
    def _():
        o_ref[...] = _layer_norm(o_ref[...], lnw_ref[...], lnb_ref[...])


def _mlp(x1, w_up, w_down, ln_w, ln_b, tm, tf):
    m = x1.shape[0]
    return pl.pallas_call(
        _mlp_kernel,
        grid=(m // tm, D_FF // tf),
        in_specs=[
            pl.BlockSpec((tm, D_MODEL), lambda i, j: (i, 0)),
            pl.BlockSpec((D_MODEL, tf), lambda i, j: (0, j)),
            pl.BlockSpec((tf, D_MODEL), lambda i, j: (j, 0)),
            pl.BlockSpec((1, D_MODEL), lambda i, j: (0, 0)),
            pl.BlockSpec((1, D_MODEL), lambda i, j: (0, 0)),
        ],
        out_specs=pl.BlockSpec((tm, D_MODEL), lambda i, j: (i, 0)),
        out_shape=jax.ShapeDtypeStruct((m, D_MODEL), F32),
        compiler_params=pltpu.CompilerParams(
            dimension_semantics=("parallel", "arbitrary"), vmem_limit_bytes=VMEM_LIMIT),
        name="mlp",
    )(x1, w_up, w_down, ln_w, ln_b)


def _layer(x, w_in, mu_shift, w0, w_lora_up, a0, a_lora_up, g_lora_up, k_k, k_a, r_k,
           rwkv_gn_w, rwkv_gn_b, ret_gn_w, w_o, ln1_w, ln1_b, w_up, w_down, ln2_w, ln2_b):
    b, t, _ = x.shape
    m = b * t
    x2d = x.reshape(m, D_MODEL)
    row = lambda p: p.reshape(1, -1)

    h = _in_proj(x2d, _prep_w_in(w_in), tm=min(IN_PROJ_ROWS, m))
    h = h.reshape(b, t, RWKV_WIDTH + RET_WIDTH)

    zeros = jnp.zeros((LORA_DECAY, D_RWKV), F32)
    wd_pad = jnp.concatenate([w_lora_up, zeros], axis=0).astype(BF16)
    wa_pad = jnp.concatenate([zeros, a_lora_up], axis=0).astype(BF16)
    y_a = _wkv7(h, row(mu_shift), row(w0), row(a0), row(k_k), row(k_a), row(r_k),
                row(rwkv_gn_w), row(rwkv_gn_b), wd_pad, wa_pad, g_lora_up.astype(BF16),
                tb=min(WKV_ROWS, t), pairs=WKV_PAIRS)

    cos, sin = _rope_table(t, rows=min(ROPE_ROWS, t))
    y_b = _retention(h, cos, sin, row(ret_gn_w), tb=min(RET_ROWS, t))

    w_o16 = w_o.astype(BF16)
    x1 = _out_proj(x2d, y_a.reshape(m, D_RWKV), y_b.reshape(m, D_RET), w_o16[:D_RWKV],
                   w_o16[D_RWKV:], row(ln1_w), row(ln1_b), tm=min(OUT_PROJ_ROWS, m))
    out = _mlp(x1, w_up.astype(BF16), w_down.astype(BF16), row(ln2_w), row(ln2_b),
               tm=min(MLP_ROWS, m), tf=MLP_COLS)
    return out.reshape(b, t, D_MODEL)


def kernel(x, w_in, mu_shift, w0, w_lora_up, a0, a_lora_up, g_lora_up, k_k, k_a, r_k,
           rwkv_gn_w, rwkv_gn_b, ret_gn_w, w_o, ln1_w, ln1_b, w_up, w_down, ln2_w, ln2_b):
    for l in range(DEPTH):
        x = _layer(x, w_in[l], mu_shift[l], w0[l], w_lora_up[l], a0[l], a_lora_up[l],
                   g_lora_up[l], k_k[l], k_a[l], r_k[l], rwkv_gn_w[l], rwkv_gn_b[l],
                   ret_gn_w[l], w_o[l], ln1_w[l], ln1_b[l], w_up[l], w_down[l],
                   ln2_w[l], ln2_b[l])
    return x
```

```python
import functools
import math

import jax
import jax.numpy as jnp
from jax import lax
from jax.experimental import pallas as pl
from jax.experimental.pallas import tpu as pltpu

F32 = jnp.float32
BF16 = jnp.bfloat16

D_MODEL = 2048
D_RWKV = 1024
D_RET = 1024
RWKV_HEAD = 64
RET_HEADS = 4
RET_HEAD = 256
RET_CHUNK = 128
LORA_DECAY = 64
LORA_A = 64
LORA_GATE = 128
D_FF = 4 * D_MODEL
ROPE_BASE = 10000.0
LN_EPS = 1e-5
RWKV_GN_EPS = 64e-5
RET_GN_EPS = 1e-6
DEPTH = 1
ALPHA = (2.0 * DEPTH) ** 0.25

RWKV_WIDTH = 3 * D_RWKV + LORA_DECAY + LORA_A + LORA_GATE
RET_WIDTH = 4 * D_RET
LANES = 128
WKV_CHUNK = 64
INV_BLOCK = 16
VMEM_LIMIT = 56 * 1024 * 1024

IN_PROJ_ROWS = 256
WKV_ROWS = 512
WKV_PAIRS = 2
RET_ROWS = 1024
ROPE_ROWS = 512
OUT_PROJ_ROWS = 512
MLP_ROWS = 512
MLP_COLS = 2048


def _mm(a, b):
    return jnp.dot(a.astype(BF16), b.astype(BF16), preferred_element_type=F32)


def _mm_nt(a, b):
    return lax.dot_general(a.astype(BF16), b.astype(BF16), (((1,), (1,)), ((), ())),
                           preferred_element_type=F32)


def _mm_tn(a, b):
    return lax.dot_general(a.astype(BF16), b.astype(BF16), (((0,), (0,)), ((), ())),
                           preferred_element_type=F32)


def _layer_norm(z, w, b):
    mu = jnp.mean(z, axis=-1, keepdims=True)
    d = z - mu
    var = jnp.mean(d * d, axis=-1, keepdims=True)
    return d * lax.rsqrt(var + LN_EPS) * w + b


def _prep_w_in_kernel(w_ref, o_ref):
    blk = pl.program_id(0)
    first_qk = RWKV_WIDTH // RET_HEAD
    is_qk = (blk >= first_qk) & (blk < first_qk + 2 * RET_HEADS)
    src = lax.broadcasted_iota(jnp.int32, (RET_HEAD, RET_HEAD), 0)
    dst = lax.broadcasted_iota(jnp.int32, (RET_HEAD, RET_HEAD), 1)
    split = (src % 2) * (RET_HEAD // 2) + src // 2
    perm = (dst == jnp.where(is_qk, split, src)).astype(BF16)
    o_ref[...] = jnp.dot(w_ref[...].astype(BF16), perm,
                         preferred_element_type=F32).astype(BF16)


def _prep_w_in(w_in):
    k, n = w_in.shape
    return pl.pallas_call(
        _prep_w_in_kernel,
        grid=(n // RET_HEAD,),
        in_specs=[pl.BlockSpec((k, RET_HEAD), lambda j: (0, j))],
        out_specs=pl.BlockSpec((k, RET_HEAD), lambda j: (0, j)),
        out_shape=jax.ShapeDtypeStruct((k, n), BF16),
        name="prep_w_in",
    )(w_in)


def _in_proj_kernel(x_ref, w_ref, h_ref):
    h_ref[...] = jnp.dot(x_ref[...].astype(BF16), w_ref[...], preferred_element_type=F32)


def _in_proj(x2d, w16, tm):
    m = x2d.shape[0]
    n = w16.shape[1]
    return pl.pallas_call(
        _in_proj_kernel,
        grid=(m // tm,),
        in_specs=[
            pl.BlockSpec((tm, D_MODEL), lambda i: (i, 0)),
            pl.BlockSpec((D_MODEL, n), lambda i: (0, 0), pipeline_mode=pl.Buffered(1)),
        ],
        out_specs=pl.BlockSpec((tm, n), lambda i: (i, 0)),
        out_shape=jax.ShapeDtypeStruct((m, n), F32),
        compiler_params=pltpu.CompilerParams(
            dimension_semantics=("parallel",), vmem_limit_bytes=VMEM_LIMIT),
        name="in_proj",
    )(x2d, w16)


def _rope_kernel(cos_ref, sin_ref, *, rows):
    i = pl.program_id(0)
    half = RET_HEAD // 2
    lane = lax.broadcasted_iota(jnp.int32, (rows, half), 1).astype(F32)
    pos = (lax.broadcasted_iota(jnp.int32, (rows, half), 0) + i * rows).astype(F32)
    inv_freq = jnp.exp(lane * (-math.log(ROPE_BASE) / (half - 1)))
    theta = pos * inv_freq
    cos_ref[...] = jnp.cos(theta)
    sin_ref[...] = jnp.sin(theta)


def _rope_table(t, rows):
    half = RET_HEAD // 2
    return pl.pallas_call(
        functools.partial(_rope_kernel, rows=rows),
        grid=(t // rows,),
        out_specs=[pl.BlockSpec((rows, half), lambda i: (i, 0)),
                   pl.BlockSpec((rows, half), lambda i: (i, 0))],
        out_shape=[jax.ShapeDtypeStruct((t, half), F32),
                   jax.ShapeDtypeStruct((t, half), F32)],
        name="rope_table",
    )()


def _token_shift(x_ref, prev_ref, mu_ref):
    x = x_ref[0]
    rows = x.shape[0]
    rolled = pltpu.roll(x, 1, axis=0)
    row = lax.broadcasted_iota(jnp.int32, x.shape, 0)
    prev = jnp.where(row == 0, prev_ref[...], rolled)
    prev_ref[...] = x[rows - 1:rows, :]
    return x + (prev - x) * mu_ref[...]


def _ipm(a, b):
    return a + b + _mm(a, b)


def _wkv_kernel(r_ref, k_ref, v_ref, lo_ref, mur_ref, muk_ref, muv_ref, mulo_ref,
                w0_ref, a0_ref, kk_ref, ka_ref, rk_ref, gnw_ref, gnb_ref,
                wd_ref, wa_ref, wg_ref, y_ref,
                st_ref, pr_ref, pk_ref, pv_ref, plo_ref,
                q_sc, m_sc, yi_sc, u_sc, d_sc, bonus_sc, g_sc, *, tb, pairs):
    L = WKV_CHUNK
    S = 2 * L
    width = pairs * LANES

    @pl.when(pl.program_id(2) == 0)
    def _():
        for ref in (st_ref, pr_ref, pk_ref, pv_ref, plo_ref,
                    q_sc, m_sc, yi_sc, u_sc, d_sc, bonus_sc, g_sc):
            ref[...] = jnp.zeros_like(ref)

    nc = tb // L
    units = [(c, p) for c in range(nc) for p in range(pairs)]
    chunks = range(len(units))

    st = [st_ref[p] for p in range(pairs)]
    ys = [[] for _ in range(pairs)]
    pending = list(range(nc))

    def phase_b(steps=1):
        for c in pending[:steps]:
            for p in range(pairs):
                u = c * pairs + p
                y_st = yi_sc[u] + _mm_nt(q_sc[u], st[p])
                ys[p].append(y_st[:L] + y_st[L:])
                st[p] = st[p] * d_sc[u] + _mm(st[p], m_sc[u]) + u_sc[u]
        del pending[:steps]

    phase_b(2)

    rf = _token_shift(r_ref, pr_ref, mur_ref)
    kf = _token_shift(k_ref, pk_ref, muk_ref)
    vf = _token_shift(v_ref, pv_ref, muv_ref)
    lof = _token_shift(lo_ref, plo_ref, mulo_ref)

    x1 = lof[:, :LANES]
    dec = w0_ref[...] + _mm(jnp.tanh(x1), wd_ref[...])
    z = -dec
    w = -(jnp.maximum(z, 0.0) + jnp.log(1.0 + jnp.exp(-jnp.abs(z)))) - 0.5
    logw = -jnp.exp(w)
    a = jax.nn.sigmoid(a0_ref[...] + _mm(x1, wa_ref[...]))
    g = _mm(jax.nn.sigmoid(lof[:, LANES:]), wg_ref[...])

    li = lax.broadcasted_iota(jnp.int32, (width, width), 0) // RWKV_HEAD
    lj = lax.broadcasted_iota(jnp.int32, (width, width), 1) // RWKV_HEAD
    head_ones = (li == lj).astype(BF16)

    def head_sum(x):
        return _mm(x, head_ones)

    kkr = kf * kk_ref[...]
    kkn = kkr / jnp.maximum(jnp.sqrt(head_sum(kkr * kkr)), 1e-12)
    k2 = kf * (1.0 + (a - 1.0) * ka_ref[...])
    bonus = head_sum(rf * k2 * rk_ref[...]) * vf
    avec = -kkn
    bvec = kkn * a

    ri = lax.broadcasted_iota(jnp.int32, (S, S), 0)
    ci = lax.broadcasted_iota(jnp.int32, (S, S), 1)
    same_head = (ri // L) == (ci // L)
    strict = same_head & (ri > ci)
    incl = same_head & (ri >= ci)
    same_blk = (ri // INV_BLOCK) == (ci // INV_BLOCK)
    strict_blk = strict & same_blk
    strict_off = strict & jnp.logical_not(same_blk)
    lane_head = lax.broadcasted_iota(jnp.int32, (L, LANES), 1) // RWKV_HEAD
    ti = lax.broadcasted_iota(jnp.int32, (L, L), 0)
    tj = lax.broadcasted_iota(jnp.int32, (L, L), 1)
    tri = (ti >= tj).astype(BF16)

    def stack(x):
        return jnp.concatenate([jnp.where(lane_head == 0, x, 0.0),
                                jnp.where(lane_head == 1, x, 0.0)], axis=0)

    def twice(x):
        return jnp.concatenate([x, x], axis=0)


    def part(x, c, p):
        return x[c * L:(c + 1) * L, p * LANES:(p + 1) * LANES]

    cum_c = [_mm_split_lhs(tri, logw[c * L:(c + 1) * L]) for c in range(nc)]
    a_s, r_s, b16, k16, v16, bh16, kh16, decay = [], [], [], [], [], [], [], []
    for c, p in units:
        lw = part(logw, c, p)
        cum = cum_c[c][:, p * LANES:(p + 1) * LANES]
        tot = cum[L - 1:L, :]
        e_neg = jnp.exp(-cum)
        e_tail = jnp.exp(tot - cum)
        bv, kv = part(bvec, c, p), part(k2, c, p)
        a_s.append(stack(part(avec, c, p) * jnp.exp(cum - lw)))
        r_s.append(stack(part(rf, c, p) * jnp.exp(cum)))
        b16.append(twice((bv * e_neg).astype(BF16)))
        k16.append(twice((kv * e_neg).astype(BF16)))
        v16.append(stack(part(vf, c, p)).astype(BF16))
        bh16.append(stack(bv * e_tail).astype(BF16))
        kh16.append(stack(kv * e_tail).astype(BF16))
        decay.append(jnp.exp(tot))
    a16 = [x.astype(BF16) for x in a_s]
    bigs = [_mm_nt(jnp.concatenate([a16[c], r_s[c].astype(BF16)], axis=0),
                   jnp.concatenate([b16[c], k16[c]], axis=0)) for c in chunks]
    nd = [jnp.where(strict_blk, big[:S, :S], 0.0) for big in bigs]
    ne = [jnp.where(strict_off, big[:S, :S], 0.0) for big in bigs]
    a_ak = [jnp.where(strict, big[:S, S:], 0.0).astype(BF16) for big in bigs]
    a_br = [jnp.where(incl, big[S:, :S], 0.0).astype(BF16) for big in bigs]
    a_kr = [jnp.where(incl, big[S:, S:], 0.0).astype(BF16) for big in bigs]
    phase_b()

    akv = [_mm(a_ak[c], v16[c]) for c in chunks]
    pw = [_mm(x, x) for x in nd]
    sm = [nd[c] + pw[c] for c in chunks]
    pw = [x.astype(BF16) for x in pw]
    phase_b()
    for _ in range(2):
        both = [_mm(pw[c], jnp.concatenate([sm[c].astype(BF16), pw[c]], axis=1)) for c in chunks]
        sm = [sm[c] + both[c][:, :S] for c in chunks]
        pw = [both[c][:, S:].astype(BF16) for c in chunks]
        phase_b()
    dm = [sm[c] + _mm(pw[c], sm[c]) for c in chunks]
    phase_b()

    krv = [_mm(a_kr[c], v16[c]) for c in chunks]
    vkh = [_mm_tn(v16[c], kh16[c]) for c in chunks]
    phase_b(len(pending))
    for p in range(pairs):
        st_ref[p] = st[p]

    y = jnp.concatenate([jnp.concatenate(yp, axis=0) for yp in ys], axis=1)
    mean = head_sum(y) * (1.0 / RWKV_HEAD)
    d = y - mean
    var = head_sum(d * d) * (1.0 / RWKV_HEAD)
    yn = d * lax.rsqrt(var + RWKV_GN_EPS) * gnw_ref[...] + gnb_ref[...]
    y_ref[0] = ((yn + bonus_sc[...]) * g_sc[...]).astype(y_ref.dtype)

    f1 = [ne[c] + _mm(dm[c], ne[c]) for c in chunks]
    f12 = [f1[c] + _mm(f1[c], f1[c]) for c in chunks]
    gm = [f1[c] + _mm(f1[c], f12[c]) for c in chunks]
    tm1 = [_ipm(gm[c], dm[c]) for c in chunks]
    zmat = [jnp.concatenate([a_s[c], akv[c]], axis=1) for c in chunks]
    xmat = [zmat[c] + _mm(tm1[c], zmat[c]) for c in chunks]
    x16 = [x.astype(BF16) for x in xmat]
    zx = [_mm(a_br[c], x16[c]) for c in chunks]
    mu_t = [_mm_tn(x16[c], bh16[c]) for c in chunks]
    q = [r_s[c] + zx[c][:, :LANES] for c in chunks]
    y_intra = [zx[c][:, LANES:] + krv[c] for c in chunks]
    m_t = [mu_t[c][:LANES] for c in chunks]
    u_t = [mu_t[c][LANES:] + vkh[c] for c in chunks]

    bonus_sc[...] = bonus
    g_sc[...] = g
    for u in chunks:
        q_sc[u] = q[u].astype(BF16)
        m_sc[u] = m_t[u].astype(BF16)
        yi_sc[u] = y_intra[u]
        u_sc[u] = u_t[u]
        d_sc[u] = decay[u]


def _mm_split_lhs(a, b):
    hi = b.astype(BF16)
    lo = (b - hi.astype(F32)).astype(BF16)
    return (jnp.dot(a, hi, preferred_element_type=F32)
            + jnp.dot(a, lo, preferred_element_type=F32))


def _wkv7(h_a, mu, w0, a0, k_k, k_a, r_k, gn_w, gn_b, wd_pad, wa_pad, wg, tb, pairs):
    b, t, _ = h_a.shape
    width = pairs * LANES
    groups = D_RWKV // width
    nt = t // tb
    units = (tb // WKV_CHUNK) * pairs
    col = lambda off: (lambda bi, p, ti: (bi, jnp.minimum(ti, nt - 1), off + p))
    vec = lambda off: (lambda bi, p, ti: (0, off + p))
    lora_blk = (3 * D_RWKV) // (2 * LANES)
    in_specs = [
        pl.BlockSpec((1, tb, width), col(0)),
        pl.BlockSpec((1, tb, width), col(groups)),
        pl.BlockSpec((1, tb, width), col(2 * groups)),
        pl.BlockSpec((1, tb, 2 * LANES),
                     lambda bi, p, ti: (bi, jnp.minimum(ti, nt - 1), lora_blk)),
        pl.BlockSpec((1, width), vec(0)),
        pl.BlockSpec((1, width), vec(groups)),
        pl.BlockSpec((1, width), vec(2 * groups)),
        pl.BlockSpec((1, 2 * LANES), lambda bi, p, ti: (0, lora_blk)),
    ] + [pl.BlockSpec((1, width), vec(0))] * 7 + [
        pl.BlockSpec((LANES, width), vec(0)),
        pl.BlockSpec((LANES, width), vec(0)),
        pl.BlockSpec((LANES, width), vec(0)),
    ]
    return pl.pallas_call(
        functools.partial(_wkv_kernel, tb=tb, pairs=pairs),
        grid=(b, groups, nt + 1),
        in_specs=in_specs,
        out_specs=pl.BlockSpec((1, tb, width),
                               lambda bi, p, ti: (bi, jnp.maximum(ti - 1, 0), p)),
        out_shape=jax.ShapeDtypeStruct((b, t, D_RWKV), BF16),
        scratch_shapes=[
            pltpu.VMEM((pairs, LANES, LANES), F32),
            pltpu.VMEM((1, width), F32),
            pltpu.VMEM((1, width), F32),
            pltpu.VMEM((1, width), F32),
            pltpu.VMEM((1, 2 * LANES), F32),
            pltpu.VMEM((units, 2 * WKV_CHUNK, LANES), BF16),
            pltpu.VMEM((units, LANES, LANES), BF16),
            pltpu.VMEM((units, 2 * WKV_CHUNK, LANES), F32),
            pltpu.VMEM((units, LANES, LANES), F32),
            pltpu.VMEM((units, 1, LANES), F32),
            pltpu.VMEM((tb, width), F32),
            pltpu.VMEM((tb, width), F32),
        ],
        compiler_params=pltpu.CompilerParams(
            dimension_semantics=("parallel", "parallel", "arbitrary"),
            vmem_limit_bytes=VMEM_LIMIT),
        name="wkv7",
    )(h_a, h_a, h_a, h_a, mu, mu, mu, mu, w0, a0, k_k, k_a, r_k, gn_w, gn_b,
      wd_pad, wa_pad, wg)


def _ret_kernel(q_ref, k_ref, v_ref, g_ref, cos_ref, sin_ref, gnw_ref, y_ref, state_ref):
    C = RET_CHUNK
    half = RET_HEAD // 2
    hd = pl.program_id(1)

    @pl.when(pl.program_id(2) == 0)
    def _():
        state_ref[...] = jnp.zeros_like(state_ref)

    log_gamma = jnp.float32(math.log(1.0 - 2.0 ** -5.0))
    for i in range(1, RET_HEADS):
        log_gamma = jnp.where(hd == i, jnp.float32(math.log(1.0 - 2.0 ** (-5.0 - i))), log_gamma)

    cos = cos_ref[...]
    sin = sin_ref[...]

    def rotary(t):
        t1, t2 = t[:, :half], t[:, half:]
        return jnp.concatenate([t1 * cos - t2 * sin, t1 * sin + t2 * cos], axis=1)

    q_all = rotary(q_ref[0])
    k_all = rotary(k_ref[0]) * (RET_HEAD ** -0.5)
    v_all = v_ref[0].astype(BF16)

    ri = lax.broadcasted_iota(jnp.int32, (C, C), 0)
    ci = lax.broadcasted_iota(jnp.int32, (C, C), 1)
    rel = (ri - ci).astype(F32)
    decay_mask = jnp.where(ri >= ci, jnp.exp(jnp.maximum(rel, 0.0) * log_gamma), 0.0)
    idx = lax.broadcasted_iota(jnp.int32, (C, 1), 0).astype(F32)
    q_decay = jnp.exp((idx + 1.0) * log_gamma)
    k_decay = jnp.exp((C - 1.0 - idx) * log_gamma)
    chunk_decay = jnp.exp(C * log_gamma)

    chunks = range(q_all.shape[0] // C)
    rows = lambda x, c: x[c * C:(c + 1) * C]
    q = [rows(q_all, c).astype(BF16) for c in chunks]
    k = [rows(k_all, c) for c in chunks]
    v = [rows(v_all, c) for c in chunks]
    scores = [_mm_nt(q[c], k[c]) * decay_mask for c in chunks]
    kv = [_mm_tn(k[c] * k_decay, v[c]) for c in chunks]
    intra = [_mm(scores[c], v[c]) for c in chunks]
    states = [state_ref[...]]
    for c in chunks:
        states.append(states[c] * chunk_decay + kv[c])
    state_ref[...] = states[-1]
    cross = [_mm(q[c], states[c]) * q_decay for c in chunks]

    y = jnp.concatenate([intra[c] + cross[c] for c in chunks], axis=0)
    mu = jnp.mean(y, axis=-1, keepdims=True)
    d = y - mu
    var = jnp.mean(d * d, axis=-1, keepdims=True)
    yn = d * lax.rsqrt(var + RET_GN_EPS) * gnw_ref[...]
    gate = g_ref[0]
    y_ref[0] = (yn * (gate * jax.nn.sigmoid(gate))).astype(y_ref.dtype)


def _retention(h_b, cos, sin, gn_w, tb):
    b, t, _ = h_b.shape
    first = RWKV_WIDTH // RET_HEAD
    col = lambda off: (lambda bi, h, n: (bi, n, first + off + h))
    return pl.pallas_call(
        _ret_kernel,
        grid=(b, RET_HEADS, t // tb),
        in_specs=[
            pl.BlockSpec((1, tb, RET_HEAD), col(0)),
            pl.BlockSpec((1, tb, RET_HEAD), col(RET_HEADS)),
            pl.BlockSpec((1, tb, RET_HEAD), col(2 * RET_HEADS)),
            pl.BlockSpec((1, tb, RET_HEAD), col(3 * RET_HEADS)),
            pl.BlockSpec((tb, RET_HEAD // 2), lambda bi, h, n: (n, 0)),
            pl.BlockSpec((tb, RET_HEAD // 2), lambda bi, h, n: (n, 0)),
            pl.BlockSpec((1, RET_HEAD), lambda bi, h, n: (0, h)),
        ],
        out_specs=pl.BlockSpec((1, tb, RET_HEAD), lambda bi, h, n: (bi, n, h)),
        out_shape=jax.ShapeDtypeStruct((b, t, D_RET), BF16),
        scratch_shapes=[pltpu.VMEM((RET_HEAD, RET_HEAD), F32)],
        compiler_params=pltpu.CompilerParams(
            dimension_semantics=("parallel", "parallel", "arbitrary")),
        name="retention",
    )(h_b, h_b, h_b, h_b, cos, sin, gn_w)


def _out_proj_kernel(x_ref, ya_ref, yb_ref, woa_ref, wob_ref, lnw_ref, lnb_ref, o_ref):
    mix = (jnp.dot(ya_ref[...], woa_ref[...], preferred_element_type=F32)
           + jnp.dot(yb_ref[...], wob_ref[...], preferred_element_type=F32))
    o_ref[...] = _layer_norm(ALPHA * x_ref[...] + mix, lnw_ref[...], lnb_ref[...])


def _out_proj(x2d, ya, yb, wo_a, wo_b, ln_w, ln_b, tm):
    m = x2d.shape[0]
    row = lambda w: pl.BlockSpec((tm, w), lambda i: (i, 0))
    full = lambda r, c: pl.BlockSpec((r, c), lambda i: (0, 0))
    return pl.pallas_call(
        _out_proj_kernel,
        grid=(m // tm,),
        in_specs=[row(D_MODEL), row(D_RWKV), row(D_RET), full(D_RWKV, D_MODEL),
                  full(D_RET, D_MODEL), full(1, D_MODEL), full(1, D_MODEL)],
        out_specs=row(D_MODEL),
        out_shape=jax.ShapeDtypeStruct((m, D_MODEL), F32),
        compiler_params=pltpu.CompilerParams(
            dimension_semantics=("parallel",), vmem_limit_bytes=VMEM_LIMIT),
        name="out_proj",
    )(x2d, ya, yb, wo_a, wo_b, ln_w, ln_b)


def _mlp_kernel(x_ref, wu_ref, wd_ref, lnw_ref, lnb_ref, o_ref):
    j = pl.program_id(1)

    def update():
        hid = jnp.dot(x_ref[...].astype(BF16), wu_ref[...], preferred_element_type=F32)
        hid = jnp.square(jnp.maximum(hid, 0.0))
        return jnp.dot(hid.astype(BF16), wd_ref[...], preferred_element_type=F32)

    @pl.when(j == 0)
    def _():
        o_ref[...] = ALPHA * x_ref[...] + update()

    @pl.when(j > 0)
    def _():
        o_ref[...] += update()

    @pl.when(j == pl.num_programs(1) - 1)
    def _():
        o_ref[...] = _layer_norm(o_ref[...], lnw_ref[...], lnb_ref[...])


def _mlp(x1, w_up, w_down, ln_w, ln_b, tm, tf):
    m = x1.shape[0]
    return pl.pallas_call(
        _mlp_kernel,
        grid=(m // tm, D_FF // tf),
        in_specs=[
            pl.BlockSpec((tm, D_MODEL), lambda i, j: (i, 0)),
            pl.BlockSpec((D_MODEL, tf), lambda i, j: (0, j)),
            pl.BlockSpec((tf, D_MODEL), lambda i, j: (j, 0)),
            pl.BlockSpec((1, D_MODEL), lambda i, j: (0, 0)),
            pl.BlockSpec((1, D_MODEL), lambda i, j: (0, 0)),
        ],
        out_specs=pl.BlockSpec((tm, D_MODEL), lambda i, j: (i, 0)),
        out_shape=jax.ShapeDtypeStruct((m, D_MODEL), F32),
        compiler_params=pltpu.CompilerParams(
            dimension_semantics=("parallel", "arbitrary"), vmem_limit_bytes=VMEM_LIMIT),
        name="mlp",
    )(x1, w_up, w_down, ln_w, ln_b)


def _layer(x, w_in, mu_shift, w0, w_lora_up, a0, a_lora_up, g_lora_up, k_k, k_a, r_k,
           rwkv_gn_w, rwkv_gn_b, ret_gn_w, w_o, ln1_w, ln1_b, w_up, w_down, ln2_w, ln2_b):
    b, t, _ = x.shape
    m = b * t
    x2d = x.reshape(m, D_MODEL)
    row = lambda p: p.reshape(1, -1)

    h = _in_proj(x2d, _prep_w_in(w_in), tm=min(IN_PROJ_ROWS, m))
    h = h.reshape(b, t, RWKV_WIDTH + RET_WIDTH)

    zeros = jnp.zeros((LORA_DECAY, D_RWKV), F32)
    wd_pad = jnp.concatenate([w_lora_up, zeros], axis=0).astype(BF16)
    wa_pad = jnp.concatenate([zeros, a_lora_up], axis=0).astype(BF16)
    y_a = _wkv7(h, row(mu_shift), row(w0), row(a0), row(k_k), row(k_a), row(r_k),
                row(rwkv_gn_w), row(rwkv_gn_b), wd_pad, wa_pad, g_lora_up.astype(BF16),
                tb=min(WKV_ROWS, t), pairs=WKV_PAIRS)

    cos, sin = _rope_table(t, rows=min(ROPE_ROWS, t))
    y_b = _retention(h, cos, sin, row(ret_gn_w), tb=min(RET_ROWS, t))

    w_o16 = w_o.astype(BF16)
    x1 = _out_proj(x2d, y_a.reshape(m, D_RWKV), y_b.reshape(m, D_RET), w_o16[:D_RWKV],
                   w_o16[D_RWKV:], row(ln1_w), row(ln1_b), tm=min(OUT_PROJ_ROWS, m))
    out = _mlp(x1, w_up.astype(BF16), w_down.astype(BF16), row(ln2_w), row(ln2_b),
               tm=min(MLP_ROWS, m), tf=MLP_COLS)
    return out.reshape(b, t, D_MODEL)


def kernel(x, w_in, mu_shift, w0, w_lora_up, a0, a_lora_up, g_lora_up, k_k, k_a, r_k,
           rwkv_gn_w, rwkv_gn_b, ret_gn_w, w_o, ln1_w, ln1_b, w_up, w_down, ln2_w, ln2_b):
    for l in range(DEPTH):
        x = _layer(x, w_in[l], mu_shift[l], w0[l], w_lora_up[l], a0[l], a_lora_up[l],
                   g_lora_up[l], k_k[l], k_a[l], r_k[l], rwkv_gn_w[l], rwkv_gn_b[l],
                   ret_gn_w[l], w_o[l], ln1_w[l], ln1_b[l], w_up[l], w_down[l],
                   ln2_w[l], ln2_b[l])
    return x
```

```python
import functools
import math

import jax
import jax.numpy as jnp
from jax import lax
from jax.experimental import pallas as pl
from jax.experimental.pallas import tpu as pltpu

F32 = jnp.float32
BF16 = jnp.bfloat16

D_MODEL = 2048
D_RWKV = 1024
D_RET = 1024
RWKV_HEAD = 64
RET_HEADS = 4
RET_HEAD = 256
RET_CHUNK = 128
LORA_DECAY = 64
LORA_A = 64
LORA_GATE = 128
D_FF = 4 * D_MODEL
ROPE_BASE = 10000.0
LN_EPS = 1e-5
RWKV_GN_EPS = 64e-5
RET_GN_EPS = 1e-6
DEPTH = 1
ALPHA = (2.0 * DEPTH) ** 0.25

RWKV_WIDTH = 3 * D_RWKV + LORA_DECAY + LORA_A + LORA_GATE
RET_WIDTH = 4 * D_RET
LANES = 128
WKV_CHUNK = 64
INV_BLOCK = 16
VMEM_LIMIT = 56 * 1024 * 1024

IN_PROJ_ROWS = 256
WKV_ROWS = 512
WKV_PAIRS = 2
RET_ROWS = 1024
ROPE_ROWS = 512
OUT_PROJ_ROWS = 512
MLP_ROWS = 512
MLP_COLS = 2048


def _mm(a, b):
    return jnp.dot(a.astype(BF16), b.astype(BF16), preferred_element_type=F32)


def _mm_nt(a, b):
    return lax.dot_general(a.astype(BF16), b.astype(BF16), (((1,), (1,)), ((), ())),
                           preferred_element_type=F32)


def _mm_tn(a, b):
    return lax.dot_general(a.astype(BF16), b.astype(BF16), (((0,), (0,)), ((), ())),
                           preferred_element_type=F32)


def _layer_norm(z, w, b):
    mu = jnp.mean(z, axis=-1, keepdims=True)
    d = z - mu
    var = jnp.mean(d * d, axis=-1, keepdims=True)
    return d * lax.rsqrt(var + LN_EPS) * w + b


def _prep_w_in_kernel(w_ref, o_ref):
    blk = pl.program_id(0)
    first_qk = RWKV_WIDTH // RET_HEAD
    is_qk = (blk >= first_qk) & (blk < first_qk + 2 * RET_HEADS)
    src = lax.broadcasted_iota(jnp.int32, (RET_HEAD, RET_HEAD), 0)
    dst = lax.broadcasted_iota(jnp.int32, (RET_HEAD, RET_HEAD), 1)
    split = (src % 2) * (RET_HEAD // 2) + src // 2
    perm = (dst == jnp.where(is_qk, split, src)).astype(BF16)
    o_ref[...] = jnp.dot(w_ref[...].astype(BF16), perm,
                         preferred_element_type=F32).astype(BF16)


def _prep_w_in(w_in):
    k, n = w_in.shape
    return pl.pallas_call(
        _prep_w_in_kernel,
        grid=(n // RET_HEAD,),
        in_specs=[pl.BlockSpec((k, RET_HEAD), lambda j: (0, j))],
        out_specs=pl.BlockSpec((k, RET_HEAD), lambda j: (0, j)),
        out_shape=jax.ShapeDtypeStruct((k, n), BF16),
        name="prep_w_in",
    )(w_in)


def _in_proj_kernel(x_ref, w_ref, h_ref):
    h_ref[...] = jnp.dot(x_ref[...].astype(BF16), w_ref[...], preferred_element_type=F32)


def _in_proj(x2d, w16, tm):
    m = x2d.shape[0]
    n = w16.shape[1]
    return pl.pallas_call(
        _in_proj_kernel,
        grid=(m // tm,),
        in_specs=[
            pl.BlockSpec((tm, D_MODEL), lambda i: (i, 0)),
            pl.BlockSpec((D_MODEL, n), lambda i: (0, 0), pipeline_mode=pl.Buffered(1)),
        ],
        out_specs=pl.BlockSpec((tm, n), lambda i: (i, 0)),
        out_shape=jax.ShapeDtypeStruct((m, n), F32),
        compiler_params=pltpu.CompilerParams(
            dimension_semantics=("parallel",), vmem_limit_bytes=VMEM_LIMIT),
        name="in_proj",
    )(x2d, w16)


def _rope_kernel(cos_ref, sin_ref, cos0_ref, sin0_ref, *, rows):
    i = pl.program_id(0)
    half = RET_HEAD // 2
    lane = lax.broadcasted_iota(jnp.int32, (1, half), 1).astype(F32)
    inv_freq = jnp.exp(lane * (-math.log(ROPE_BASE) / (half - 1)))

    @pl.when(i == 0)
    def _():
        pos = lax.broadcasted_iota(jnp.int32, (rows, half), 0).astype(F32)
        theta = pos * inv_freq
        cos0_ref[...] = jnp.cos(theta)
        sin0_ref[...] = jnp.sin(theta)

    base = (i * rows).astype(F32) * inv_freq
    cb, sb = jnp.cos(base), jnp.sin(base)
    cos_ref[...] = cos0_ref[...] * cb - sin0_ref[...] * sb
    sin_ref[...] = sin0_ref[...] * cb + cos0_ref[...] * sb


def _rope_table(t, rows):
    half = RET_HEAD // 2
    return pl.pallas_call(
        functools.partial(_rope_kernel, rows=rows),
        grid=(t // rows,),
        out_specs=[pl.BlockSpec((rows, half), lambda i: (i, 0)),
                   pl.BlockSpec((rows, half), lambda i: (i, 0))],
        out_shape=[jax.ShapeDtypeStruct((t, half), F32),
                   jax.ShapeDtypeStruct((t, half), F32)],
        scratch_shapes=[pltpu.VMEM((rows, half), F32), pltpu.VMEM((rows, half), F32)],
        compiler_params=pltpu.CompilerParams(dimension_semantics=("arbitrary",)),
        name="rope_table",
    )()


def _token_shift(x_ref, prev_ref, mu_ref):
    x = x_ref[0]
    rows = x.shape[0]
    rolled = pltpu.roll(x, 1, axis=0)
    row = lax.broadcasted_iota(jnp.int32, x.shape, 0)
    prev = jnp.where(row == 0, prev_ref[...], rolled)
    prev_ref[...] = x[rows - 1:rows, :]
    return x + (prev - x) * mu_ref[...]


def _ipm(a, b):
    return a + b + _mm(a, b)


def _wkv_kernel(r_ref, k_ref, v_ref, lo_ref, mur_ref, muk_ref, muv_ref, mulo_ref,
                w0_ref, a0_ref, kk_ref, ka_ref, rk_ref, gnw_ref, gnb_ref,
                wd_ref, wa_ref, wg_ref, y_ref,
                st_ref, pr_ref, pk_ref, pv_ref, plo_ref,
                q_sc, m_sc, yi_sc, u_sc, d_sc, bonus_sc, g_sc, *, tb, pairs):
    L = WKV_CHUNK
    S = 2 * L
    width = pairs * LANES

    @pl.when(pl.program_id(2) == 0)
    def _():
        for ref in (st_ref, pr_ref, pk_ref, pv_ref, plo_ref,
                    q_sc, m_sc, yi_sc, u_sc, d_sc, bonus_sc, g_sc):
            ref[...] = jnp.zeros_like(ref)

    nc = tb // L
    units = [(c, p) for c in range(nc) for p in range(pairs)]
    chunks = range(len(units))

    st = [st_ref[p] for p in range(pairs)]
    ys = [[] for _ in range(pairs)]
    pending = list(range(nc))

    def phase_b(steps=1):
        for c in pending[:steps]:
            for p in range(pairs):
                u = c * pairs + p
                y_st = yi_sc[u] + _mm_nt(q_sc[u], st[p])
                ys[p].append(y_st[:L] + y_st[L:])
                st[p] = st[p] * d_sc[u] + _mm(st[p], m_sc[u]) + u_sc[u]
        del pending[:steps]

    phase_b(2)

    rf = _token_shift(r_ref, pr_ref, mur_ref)
    kf = _token_shift(k_ref, pk_ref, muk_ref)
    vf = _token_shift(v_ref, pv_ref, muv_ref)
    lof = _token_shift(lo_ref, plo_ref, mulo_ref)

    x1 = lof[:, :LANES]
    dec = w0_ref[...] + _mm(jnp.tanh(x1), wd_ref[...])
    z = -dec
    w = -(jnp.maximum(z, 0.0) + jnp.log(1.0 + jnp.exp(-jnp.abs(z)))) - 0.5
    logw = -jnp.exp(w)
    a = jax.nn.sigmoid(a0_ref[...] + _mm(x1, wa_ref[...]))
    g = _mm(jax.nn.sigmoid(lof[:, LANES:]), wg_ref[...])

    li = lax.broadcasted_iota(jnp.int32, (width, width), 0) // RWKV_HEAD
    lj = lax.broadcasted_iota(jnp.int32, (width, width), 1) // RWKV_HEAD
    head_ones = (li == lj).astype(BF16)

    def head_sum(x):
        return _mm(x, head_ones)

    kkr = kf * kk_ref[...]
    kkn = kkr / jnp.maximum(jnp.sqrt(head_sum(kkr * kkr)), 1e-12)
    k2 = kf * (1.0 + (a - 1.0) * ka_ref[...])
    bonus = head_sum(rf * k2 * rk_ref[...]) * vf
    avec = -kkn
    bvec = kkn * a

    ri = lax.broadcasted_iota(jnp.int32, (S, S), 0)
    ci = lax.broadcasted_iota(jnp.int32, (S, S), 1)
    same_head = (ri // L) == (ci // L)
    strict = same_head & (ri > ci)
    incl = same_head & (ri >= ci)
    same_blk = (ri // INV_BLOCK) == (ci // INV_BLOCK)
    strict_blk = strict & same_blk
    strict_off = strict & jnp.logical_not(same_blk)
    lane_head = lax.broadcasted_iota(jnp.int32, (L, LANES), 1) // RWKV_HEAD
    ti = lax.broadcasted_iota(jnp.int32, (L, L), 0)
    tj = lax.broadcasted_iota(jnp.int32, (L, L), 1)
    tri = (ti >= tj).astype(BF16)

    def stack(x):
        return jnp.concatenate([jnp.where(lane_head == 0, x, 0.0),
                                jnp.where(lane_head == 1, x, 0.0)], axis=0)

    def twice(x):
        return jnp.concatenate([x, x], axis=0)


    def part(x, c, p):
        return x[c * L:(c + 1) * L, p * LANES:(p + 1) * LANES]

    cum_c = [_mm_split_lhs(tri, logw[c * L:(c + 1) * L]) for c in range(nc)]
    a_s, r_s, b16, k16, v16, bh16, kh16, decay = [], [], [], [], [], [], [], []
    for c, p in units:
        lw = part(logw, c, p)
        cum = cum_c[c][:, p * LANES:(p + 1) * LANES]
        tot = cum[L - 1:L, :]
        e_neg = jnp.exp(-cum)
        e_tail = jnp.exp(tot - cum)
        bv, kv = part(bvec, c, p), part(k2, c, p)
        a_s.append(stack(part(avec, c, p) * jnp.exp(cum - lw)))
        r_s.append(stack(part(rf, c, p) * jnp.exp(cum)))
        b16.append(twice((bv * e_neg).astype(BF16)))
        k16.append(twice((kv * e_neg).astype(BF16)))
        v16.append(stack(part(vf, c, p)).astype(BF16))
        bh16.append(stack(bv * e_tail).astype(BF16))
        kh16.append(stack(kv * e_tail).astype(BF16))
        decay.append(jnp.exp(tot))
    a16 = [x.astype(BF16) for x in a_s]
    bigs = [_mm_nt(jnp.concatenate([a16[c], r_s[c].astype(BF16)], axis=0),
                   jnp.concatenate([b16[c], k16[c]], axis=0)) for c in chunks]
    nd = [jnp.where(strict_blk, big[:S, :S], 0.0) for big in bigs]
    ne = [jnp.where(strict_off, big[:S, :S], 0.0) for big in bigs]
    a_ak = [jnp.where(strict, big[:S, S:], 0.0).astype(BF16) for big in bigs]
    a_br = [jnp.where(incl, big[S:, :S], 0.0).astype(BF16) for big in bigs]
    a_kr = [jnp.where(incl, big[S:, S:], 0.0).astype(BF16) for big in bigs]
    phase_b()

    akv = [_mm(a_ak[c], v16[c]) for c in chunks]
    pw = [_mm(x, x) for x in nd]
    sm = [nd[c] + pw[c] for c in chunks]
    pw = [x.astype(BF16) for x in pw]
    phase_b()
    for _ in range(2):
        both = [_mm(pw[c], jnp.concatenate([sm[c].astype(BF16), pw[c]], axis=1)) for c in chunks]
        sm = [sm[c] + both[c][:, :S] for c in chunks]
        pw = [both[c][:, S:].astype(BF16) for c in chunks]
        phase_b()
    dm = [sm[c] + _mm(pw[c], sm[c]) for c in chunks]
    phase_b()

    krv = [_mm(a_kr[c], v16[c]) for c in chunks]
    vkh = [_mm_tn(v16[c], kh16[c]) for c in chunks]
    phase_b(len(pending))
    for p in range(pairs):
        st_ref[p] = st[p]

    y = jnp.concatenate([jnp.concatenate(yp, axis=0) for yp in ys], axis=1)
    mean = head_sum(y) * (1.0 / RWKV_HEAD)
    d = y - mean
    var = head_sum(d * d) * (1.0 / RWKV_HEAD)
    yn = d * lax.rsqrt(var + RWKV_GN_EPS) * gnw_ref[...] + gnb_ref[...]
    y_ref[0] = ((yn + bonus_sc[...]) * g_sc[...]).astype(y_ref.dtype)

    f1 = [ne[c] + _mm(dm[c], ne[c]) for c in chunks]
    f12 = [f1[c] + _mm(f1[c], f1[c]) for c in chunks]
    gm = [f1[c] + _mm(f1[c], f12[c]) for c in chunks]
    tm1 = [_ipm(gm[c], dm[c]) for c in chunks]
    zmat = [jnp.concatenate([a_s[c], akv[c]], axis=1) for c in chunks]
    xmat = [zmat[c] + _mm(tm1[c], zmat[c]) for c in chunks]
    x16 = [x.astype(BF16) for x in xmat]
    zx = [_mm(a_br[c], x16[c]) for c in chunks]
    mu_t = [_mm_tn(x16[c], bh16[c]) for c in chunks]
    q = [r_s[c] + zx[c][:, :LANES] for c in chunks]
    y_intra = [zx[c][:, LANES:] + krv[c] for c in chunks]
    m_t = [mu_t[c][:LANES] for c in chunks]
    u_t = [mu_t[c][LANES:] + vkh[c] for c in chunks]

    bonus_sc[...] = bonus
    g_sc[...] = g
    for u in chunks:
        q_sc[u] = q[u].astype(BF16)
        m_sc[u] = m_t[u].astype(BF16)
        yi_sc[u] = y_intra[u]
        u_sc[u] = u_t[u]
        d_sc[u] = decay[u]


def _mm_split_lhs(a, b):
    hi = b.astype(BF16)
    lo = (b - hi.astype(F32)).astype(BF16)
    return (jnp.dot(a, hi, preferred_element_type=F32)
            + jnp.dot(a, lo, preferred_element_type=F32))


def _wkv7(h_a, mu, w0, a0, k_k, k_a, r_k, gn_w, gn_b, wd_pad, wa_pad, wg, tb, pairs):
    b, t, _ = h_a.shape
    width = pairs * LANES
    groups = D_RWKV // width
    nt = t // tb
    units = (tb // WKV_CHUNK) * pairs
    col = lambda off: (lambda bi, p, ti: (bi, jnp.minimum(ti, nt - 1), off + p))
    vec = lambda off: (lambda bi, p, ti: (0, off + p))
    lora_blk = (3 * D_RWKV) // (2 * LANES)
    in_specs = [
        pl.BlockSpec((1, tb, width), col(0)),
        pl.BlockSpec((1, tb, width), col(groups)),
        pl.BlockSpec((1, tb, width), col(2 * groups)),
        pl.BlockSpec((1, tb, 2 * LANES),
                     lambda bi, p, ti: (bi, jnp.minimum(ti, nt - 1), lora_blk)),
        pl.BlockSpec((1, width), vec(0)),
        pl.BlockSpec((1, width), vec(groups)),
        pl.BlockSpec((1, width), vec(2 * groups)),
        pl.BlockSpec((1, 2 * LANES), lambda bi, p, ti: (0, lora_blk)),
    ] + [pl.BlockSpec((1, width), vec(0))] * 7 + [
        pl.BlockSpec((LANES, width), vec(0)),
        pl.BlockSpec((LANES, width), vec(0)),
        pl.BlockSpec((LANES, width), vec(0)),
    ]
    return pl.pallas_call(
        functools.partial(_wkv_kernel, tb=tb, pairs=pairs),
        grid=(b, groups, nt + 1),
        in_specs=in_specs,
        out_specs=pl.BlockSpec((1, tb, width),
                               lambda bi, p, ti: (bi, jnp.maximum(ti - 1, 0), p)),
        out_shape=jax.ShapeDtypeStruct((b, t, D_RWKV), BF16),
        scratch_shapes=[
            pltpu.VMEM((pairs, LANES, LANES), F32),
            pltpu.VMEM((1, width), F32),
            pltpu.VMEM((1, width), F32),
            pltpu.VMEM((1, width), F32),
            pltpu.VMEM((1, 2 * LANES), F32),
            pltpu.VMEM((units, 2 * WKV_CHUNK, LANES), BF16),
            pltpu.VMEM((units, LANES, LANES), BF16),
            pltpu.VMEM((units, 2 * WKV_CHUNK, LANES), F32),
            pltpu.VMEM((units, LANES, LANES), F32),
            pltpu.VMEM((units, 1, LANES), F32),
            pltpu.VMEM((tb, width), F32),
            pltpu.VMEM((tb, width), F32),
        ],
        compiler_params=pltpu.CompilerParams(
            dimension_semantics=("parallel", "parallel", "arbitrary"),
            vmem_limit_bytes=VMEM_LIMIT),
        name="wkv7",
    )(h_a, h_a, h_a, h_a, mu, mu, mu, mu, w0, a0, k_k, k_a, r_k, gn_w, gn_b,
      wd_pad, wa_pad, wg)


def _ret_kernel(q_ref, k_ref, v_ref, g_ref, cos_ref, sin_ref, gnw_ref, y_ref, state_ref):
    C = RET_CHUNK
    half = RET_HEAD // 2
    hd = pl.program_id(1)

    @pl.when(pl.program_id(2) == 0)
    def _():
        state_ref[...] = jnp.zeros_like(state_ref)

    log_gamma = jnp.float32(math.log(1.0 - 2.0 ** -5.0))
    for i in range(1, RET_HEADS):
        log_gamma = jnp.where(hd == i, jnp.float32(math.log(1.0 - 2.0 ** (-5.0 - i))), log_gamma)

    cos = cos_ref[...]
    sin = sin_ref[...]

    def rotary(t):
        t1, t2 = t[:, :half], t[:, half:]
        return jnp.concatenate([t1 * cos - t2 * sin, t1 * sin + t2 * cos], axis=1)

    q_all = rotary(q_ref[0])
    k_all = rotary(k_ref[0]) * (RET_HEAD ** -0.5)
    v_all = v_ref[0].astype(BF16)

    ri = lax.broadcasted_iota(jnp.int32, (C, C), 0)
    ci = lax.broadcasted_iota(jnp.int32, (C, C), 1)
    rel = (ri - ci).astype(F32)
    decay_mask = jnp.where(ri >= ci, jnp.exp(jnp.maximum(rel, 0.0) * log_gamma), 0.0)
    idx = lax.broadcasted_iota(jnp.int32, (C, 1), 0).astype(F32)
    q_decay = jnp.exp((idx + 1.0) * log_gamma)
    k_decay = jnp.exp((C - 1.0 - idx) * log_gamma)
    chunk_decay = jnp.exp(C * log_gamma)

    chunks = range(q_all.shape[0] // C)
    rows = lambda x, c: x[c * C:(c + 1) * C]
    q = [rows(q_all, c).astype(BF16) for c in chunks]
    k = [rows(k_all, c) for c in chunks]
    v = [rows(v_all, c) for c in chunks]
    scores = [_mm_nt(q[c], k[c]) * decay_mask for c in chunks]
    kv = [_mm_tn(k[c] * k_decay, v[c]) for c in chunks]
    intra = [_mm(scores[c], v[c]) for c in chunks]
    states = [state_ref[...]]
    for c in chunks:
        states.append(states[c] * chunk_decay + kv[c])
    state_ref[...] = states[-1]
    cross = [_mm(q[c], states[c]) * q_decay for c in chunks]

    y = jnp.concatenate([intra[c] + cross[c] for c in chunks], axis=0)
    mu = jnp.mean(y, axis=-1, keepdims=True)
    d = y - mu
    var = jnp.mean(d * d, axis=-1, keepdims=True)
    yn = d * lax.rsqrt(var + RET_GN_EPS) * gnw_ref[...]
    gate = g_ref[0]
    y_ref[0] = (yn * (gate * jax.nn.sigmoid(gate))).astype(y_ref.dtype)


def _retention(h_b, cos, sin, gn_w, tb):
    b, t, _ = h_b.shape
    first = RWKV_WIDTH // RET_HEAD
    col = lambda off: (lambda bi, h, n: (bi, n, first + off + h))
    return pl.pallas_call(
        _ret_kernel,
        grid=(b, RET_HEADS, t // tb),
        in_specs=[
            pl.BlockSpec((1, tb, RET_HEAD), col(0)),
            pl.BlockSpec((1, tb, RET_HEAD), col(RET_HEADS)),
            pl.BlockSpec((1, tb, RET_HEAD), col(2 * RET_HEADS)),
            pl.BlockSpec((1, tb, RET_HEAD), col(3 * RET_HEADS)),
            pl.BlockSpec((tb, RET_HEAD // 2), lambda bi, h, n: (n, 0)),
            pl.BlockSpec((tb, RET_HEAD // 2), lambda bi, h, n: (n, 0)),
            pl.BlockSpec((1, RET_HEAD), lambda bi, h, n: (0, h)),
        ],
        out_specs=pl.BlockSpec((1, tb, RET_HEAD), lambda bi, h, n: (bi, n, h)),
        out_shape=jax.ShapeDtypeStruct((b, t, D_RET), BF16),
        scratch_shapes=[pltpu.VMEM((RET_HEAD, RET_HEAD), F32)],
        compiler_params=pltpu.CompilerParams(
            dimension_semantics=("parallel", "parallel", "arbitrary")),
        name="retention",
    )(h_b, h_b, h_b, h_b, cos, sin, gn_w)


def _out_proj_kernel(x_ref, ya_ref, yb_ref, woa_ref, wob_ref, lnw_ref, lnb_ref, o_ref):
    parts = 4
    rows = x_ref.shape[0] // parts

    def mix(r):
        sl = pl.ds(r * rows, rows)
        return (jnp.dot(ya_ref[sl, :], woa_ref[...], preferred_element_type=F32)
                + jnp.dot(yb_ref[sl, :], wob_ref[...], preferred_element_type=F32))

    def finish(r, m):
        sl = pl.ds(r * rows, rows)
        o_ref[sl, :] = _layer_norm(ALPHA * x_ref[sl, :] + m, lnw_ref[...], lnb_ref[...])

    pending = mix(0)
    for r in range(1, parts):
        nxt = mix(r)
        finish(r - 1, pending)
        pending = nxt
    finish(parts - 1, pending)


def _out_proj(x2d, ya, yb, wo_a, wo_b, ln_w, ln_b, tm):
    m = x2d.shape[0]
    row = lambda w: pl.BlockSpec((tm, w), lambda i: (i, 0))
    full = lambda r, c: pl.BlockSpec((r, c), lambda i: (0, 0))
    return pl.pallas_call(
        _out_proj_kernel,
        grid=(m // tm,),
        in_specs=[row(D_MODEL), row(D_RWKV), row(D_RET), full(D_RWKV, D_MODEL),
                  full(D_RET, D_MODEL), full(1, D_MODEL), full(1, D_MODEL)],
        out_specs=row(D_MODEL),
        out_shape=jax.ShapeDtypeStruct((m, D_MODEL), F32),
        compiler_params=pltpu.CompilerParams(
            dimension_semantics=("parallel",), vmem_limit_bytes=VMEM_LIMIT),
        name="out_proj",
    )(x2d, ya, yb, wo_a, wo_b, ln_w, ln_b)


def _mlp_kernel(x_ref, wu_ref, wd_ref, lnw_ref, lnb_ref, o_ref):
    j = pl.program_id(1)

    def update():
        hid = jnp.dot(x_ref[...].astype(BF16), wu_ref[...], preferred_element_type=F32)
        hid = jnp.square(jnp.maximum(hid, 0.0))
        return jnp.dot(hid.astype(BF16), wd_ref[...], preferred_element_type=F32)

    @pl.when(j == 0)
    def _():
        o_ref[...] = ALPHA * x_ref[...] + update()

    @pl.when(j > 0)
    def _():
        o_ref[...] += update()

    @pl.when(j == pl.num_programs(1) - 1)
    def _():
        o_ref[...] = _layer_norm(o_ref[...], lnw_ref[...], lnb_ref[...])


def _mlp(x1, w_up, w_down, ln_w, ln_b, tm, tf):
    m = x1.shape[0]
    return pl.pallas_call(
        _mlp_kernel,
        grid=(m // tm, D_FF // tf),
        in_specs=[
            pl.BlockSpec((tm, D_MODEL), lambda i, j: (i, 0)),
            pl.BlockSpec((D_MODEL, tf), lambda i, j: (0, j)),
            pl.BlockSpec((tf, D_MODEL), lambda i, j: (j, 0)),
            pl.BlockSpec((1, D_MODEL), lambda i, j: (0, 0)),
            pl.BlockSpec((1, D_MODEL), lambda i, j: (0, 0)),
        ],
        out_specs=pl.BlockSpec((tm, D_MODEL), lambda i, j: (i, 0)),
        out_shape=jax.ShapeDtypeStruct((m, D_MODEL), F32),
        compiler_params=pltpu.CompilerParams(
            dimension_semantics=("parallel", "arbitrary"), vmem_limit_bytes=VMEM_LIMIT),
        name="mlp",
    )(x1, w_up, w_down, ln_w, ln_b)


def _layer(x, w_in, mu_shift, w0, w_lora_up, a0, a_lora_up, g_lora_up, k_k, k_a, r_k,
           rwkv_gn_w, rwkv_gn_b, ret_gn_w, w_o, ln1_w, ln1_b, w_up, w_down, ln2_w, ln2_b):
    b, t, _ = x.shape
    m = b * t
    x2d = x.reshape(m, D_MODEL)
    row = lambda p: p.reshape(1, -1)

    h = _in_proj(x2d, _prep_w_in(w_in), tm=min(IN_PROJ_ROWS, m))
    h = h.reshape(b, t, RWKV_WIDTH + RET_WIDTH)

    zeros = jnp.zeros((LORA_DECAY, D_RWKV), F32)
    wd_pad = jnp.concatenate([w_lora_up, zeros], axis=0).astype(BF16)
    wa_pad = jnp.concatenate([zeros, a_lora_up], axis=0).astype(BF16)
    y_a = _wkv7(h, row(mu_shift), row(w0), row(a0), row(k_k), row(k_a), row(r_k),
                row(rwkv_gn_w), row(rwkv_gn_b), wd_pad, wa_pad, g_lora_up.astype(BF16),
                tb=min(WKV_ROWS, t), pairs=WKV_PAIRS)

    cos, sin = _rope_table(t, rows=min(ROPE_ROWS, t))
    y_b = _retention(h, cos, sin, row(ret_gn_w), tb=min(RET_ROWS, t))

    w_o16 = w_o.astype(BF16)
    x1 = _out_proj(x2d, y_a.reshape(m, D_RWKV), y_b.reshape(m, D_RET), w_o16[:D_RWKV],
                   w_o16[D_RWKV:], row(ln1_w), row(ln1_b), tm=min(OUT_PROJ_ROWS, m))
    out = _mlp(x1, w_up.astype(BF16), w_down.astype(BF16), row(ln2_w), row(ln2_b),
               tm=min(MLP_ROWS, m), tf=MLP_COLS)
    return out.reshape(b, t, D_MODEL)


def kernel(x, w_in, mu_shift, w0, w_lora_up, a0, a_lora_up, g_lora_up, k_k, k_a, r_k,
           rwkv_gn_w, rwkv_gn_b, ret_gn_w, w_o, ln1_w, ln1_b, w_up, w_down, ln2_w, ln2_b):
    for l in range(DEPTH):
        x = _layer(x, w_in[l], mu_shift[l], w0[l], w_lora_up[l], a0[l], a_lora_up[l],
                   g_lora_up[l], k_k[l], k_a[l], r_k[l], rwkv_gn_w[l], rwkv_gn_b[l],
                   ret_gn_w[l], w_o[l], ln1_w[l], ln1_b[l], w_up[l], w_down[l],
                   ln2_w[l], ln2_b[l])
    return x
```

```python
import functools
import math

import jax
import jax.numpy as jnp
from jax import lax
from jax.experimental import pallas as pl
from jax.experimental.pallas import tpu as pltpu

F32 = jnp.float32
BF16 = jnp.bfloat16

D_MODEL = 2048
D_RWKV = 1024
D_RET = 1024
RWKV_HEAD = 64
RET_HEADS = 4
RET_HEAD = 256
RET_CHUNK = 128
LORA_DECAY = 64
LORA_A = 64
LORA_GATE = 128
D_FF = 4 * D_MODEL
ROPE_BASE = 10000.0
LN_EPS = 1e-5
RWKV_GN_EPS = 64e-5
RET_GN_EPS = 1e-6
DEPTH = 1
ALPHA = (2.0 * DEPTH) ** 0.25

RWKV_WIDTH = 3 * D_RWKV + LORA_DECAY + LORA_A + LORA_GATE
RET_WIDTH = 4 * D_RET
LANES = 128
WKV_CHUNK = 64
INV_BLOCK = 16
VMEM_LIMIT = 56 * 1024 * 1024

IN_PROJ_ROWS = 256
WKV_ROWS = 512
WKV_PAIRS = 2
RET_ROWS = 1024
ROPE_ROWS = 512
OUT_PROJ_ROWS = 512
MLP_ROWS = 512
MLP_COLS = 2048


def _mm(a, b):
    return jnp.dot(a.astype(BF16), b.astype(BF16), preferred_element_type=F32)


def _mm_nt(a, b):
    return lax.dot_general(a.astype(BF16), b.astype(BF16), (((1,), (1,)), ((), ())),
                           preferred_element_type=F32)


def _mm_tn(a, b):
    return lax.dot_general(a.astype(BF16), b.astype(BF16), (((0,), (0,)), ((), ())),
                           preferred_element_type=F32)


def _layer_norm(z, w, b):
    mu = jnp.mean(z, axis=-1, keepdims=True)
    d = z - mu
    var = jnp.mean(d * d, axis=-1, keepdims=True)
    return d * lax.rsqrt(var + LN_EPS) * w + b


def _prep_w_in_kernel(w_ref, o_ref):
    blk = pl.program_id(0)
    first_qk = RWKV_WIDTH // RET_HEAD
    is_qk = (blk >= first_qk) & (blk < first_qk + 2 * RET_HEADS)
    src = lax.broadcasted_iota(jnp.int32, (RET_HEAD, RET_HEAD), 0)
    dst = lax.broadcasted_iota(jnp.int32, (RET_HEAD, RET_HEAD), 1)
    split = (src % 2) * (RET_HEAD // 2) + src // 2
    perm = (dst == jnp.where(is_qk, split, src)).astype(BF16)
    o_ref[...] = jnp.dot(w_ref[...].astype(BF16), perm,
                         preferred_element_type=F32).astype(BF16)


def _prep_w_in(w_in):
    k, n = w_in.shape
    return pl.pallas_call(
        _prep_w_in_kernel,
        grid=(n // RET_HEAD,),
        in_specs=[pl.BlockSpec((k, RET_HEAD), lambda j: (0, j))],
        out_specs=pl.BlockSpec((k, RET_HEAD), lambda j: (0, j)),
        out_shape=jax.ShapeDtypeStruct((k, n), BF16),
        name="prep_w_in",
    )(w_in)


def _in_proj_kernel(x_ref, w_ref, ha_ref, hb_ref):
    xb = x_ref[...].astype(BF16)
    ha_ref[...] = jnp.dot(xb, w_ref[:, :RWKV_WIDTH], preferred_element_type=F32)
    hb_ref[...] = jnp.dot(xb, w_ref[:, RWKV_WIDTH:], preferred_element_type=F32).astype(BF16)


def _in_proj(x2d, w16, tm):
    m = x2d.shape[0]
    n = w16.shape[1]
    return pl.pallas_call(
        _in_proj_kernel,
        grid=(m // tm,),
        in_specs=[
            pl.BlockSpec((tm, D_MODEL), lambda i: (i, 0)),
            pl.BlockSpec((D_MODEL, n), lambda i: (0, 0), pipeline_mode=pl.Buffered(1)),
        ],
        out_specs=[pl.BlockSpec((tm, RWKV_WIDTH), lambda i: (i, 0)),
                   pl.BlockSpec((tm, RET_WIDTH), lambda i: (i, 0))],
        out_shape=[jax.ShapeDtypeStruct((m, RWKV_WIDTH), F32),
                   jax.ShapeDtypeStruct((m, RET_WIDTH), BF16)],
        compiler_params=pltpu.CompilerParams(
            dimension_semantics=("parallel",), vmem_limit_bytes=VMEM_LIMIT),
        name="in_proj",
    )(x2d, w16)


def _rope_kernel(cos_ref, sin_ref, cos0_ref, sin0_ref, *, rows):
    i = pl.program_id(0)
    half = RET_HEAD // 2
    lane = lax.broadcasted_iota(jnp.int32, (1, half), 1).astype(F32)
    inv_freq = jnp.exp(lane * (-math.log(ROPE_BASE) / (half - 1)))

    @pl.when(i == 0)
    def _():
        pos = lax.broadcasted_iota(jnp.int32, (rows, half), 0).astype(F32)
        theta = pos * inv_freq
        cos0_ref[...] = jnp.cos(theta)
        sin0_ref[...] = jnp.sin(theta)

    base = (i * rows).astype(F32) * inv_freq
    cb, sb = jnp.cos(base), jnp.sin(base)
    cos_ref[...] = cos0_ref[...] * cb - sin0_ref[...] * sb
    sin_ref[...] = sin0_ref[...] * cb + cos0_ref[...] * sb


def _rope_table(t, rows):
    half = RET_HEAD // 2
    return pl.pallas_call(
        functools.partial(_rope_kernel, rows=rows),
        grid=(t // rows,),
        out_specs=[pl.BlockSpec((rows, half), lambda i: (i, 0)),
                   pl.BlockSpec((rows, half), lambda i: (i, 0))],
        out_shape=[jax.ShapeDtypeStruct((t, half), F32),
                   jax.ShapeDtypeStruct((t, half), F32)],
        scratch_shapes=[pltpu.VMEM((rows, half), F32), pltpu.VMEM((rows, half), F32)],
        compiler_params=pltpu.CompilerParams(dimension_semantics=("arbitrary",)),
        name="rope_table",
    )()


def _token_shift(x_ref, prev_ref, mu_ref):
    x = x_ref[0]
    rows = x.shape[0]
    rolled = pltpu.roll(x, 1, axis=0)
    row = lax.broadcasted_iota(jnp.int32, x.shape, 0)
    prev = jnp.where(row == 0, prev_ref[...], rolled)
    prev_ref[...] = x[rows - 1:rows, :]
    return x + (prev - x) * mu_ref[...]


def _ipm(a, b):
    return a + b + _mm(a, b)


def _wkv_kernel(r_ref, k_ref, v_ref, lo_ref, mur_ref, muk_ref, muv_ref, mulo_ref,
                w0_ref, a0_ref, kk_ref, ka_ref, rk_ref, gnw_ref, gnb_ref,
                wd_ref, wa_ref, wg_ref, y_ref,
                st_ref, pr_ref, pk_ref, pv_ref, plo_ref,
                q_sc, m_sc, yi_sc, u_sc, d_sc, bonus_sc, g_sc, *, tb, pairs):
    L = WKV_CHUNK
    S = 2 * L
    width = pairs * LANES

    @pl.when(pl.program_id(2) == 0)
    def _():
        for ref in (st_ref, pr_ref, pk_ref, pv_ref, plo_ref,
                    q_sc, m_sc, yi_sc, u_sc, d_sc, bonus_sc, g_sc):
            ref[...] = jnp.zeros_like(ref)

    nc = tb // L
    units = [(c, p) for c in range(nc) for p in range(pairs)]
    chunks = range(len(units))

    st = [st_ref[p] for p in range(pairs)]
    ys = [[] for _ in range(pairs)]
    pending = list(range(nc))

    def phase_b(steps=1):
        for c in pending[:steps]:
            for p in range(pairs):
                u = c * pairs + p
                y_st = yi_sc[u] + _mm_nt(q_sc[u], st[p])
                ys[p].append(y_st[:L] + y_st[L:])
                st[p] = st[p] * d_sc[u] + _mm(st[p], m_sc[u]) + u_sc[u]
        del pending[:steps]

    phase_b(2)

    rf = _token_shift(r_ref, pr_ref, mur_ref)
    kf = _token_shift(k_ref, pk_ref, muk_ref)
    vf = _token_shift(v_ref, pv_ref, muv_ref)
    lof = _token_shift(lo_ref, plo_ref, mulo_ref)

    x1 = lof[:, :LANES]
    dec = w0_ref[...] + _mm(jnp.tanh(x1), wd_ref[...])
    z = -dec
    w = -(jnp.maximum(z, 0.0) + jnp.log(1.0 + jnp.exp(-jnp.abs(z)))) - 0.5
    logw = -jnp.exp(w)
    a = jax.nn.sigmoid(a0_ref[...] + _mm(x1, wa_ref[...]))
    g = _mm(jax.nn.sigmoid(lof[:, LANES:]), wg_ref[...])

    li = lax.broadcasted_iota(jnp.int32, (width, width), 0) // RWKV_HEAD
    lj = lax.broadcasted_iota(jnp.int32, (width, width), 1) // RWKV_HEAD
    head_ones = (li == lj).astype(BF16)

    def head_sum(x):
        return _mm(x, head_ones)

    kkr = kf * kk_ref[...]
    kkn = kkr / jnp.maximum(jnp.sqrt(head_sum(kkr * kkr)), 1e-12)
    k2 = kf * (1.0 + (a - 1.0) * ka_ref[...])
    bonus = head_sum(rf * k2 * rk_ref[...]) * vf
    avec = -kkn
    bvec = kkn * a

    ri = lax.broadcasted_iota(jnp.int32, (S, S), 0)
    ci = lax.broadcasted_iota(jnp.int32, (S, S), 1)
    same_head = (ri // L) == (ci // L)
    strict = same_head & (ri > ci)
    incl = same_head & (ri >= ci)
    same_blk = (ri // INV_BLOCK) == (ci // INV_BLOCK)
    strict_blk = strict & same_blk
    strict_off = strict & jnp.logical_not(same_blk)
    lane_head = lax.broadcasted_iota(jnp.int32, (L, LANES), 1) // RWKV_HEAD
    ti = lax.broadcasted_iota(jnp.int32, (L, L), 0)
    tj = lax.broadcasted_iota(jnp.int32, (L, L), 1)
    tri = (ti >= tj).astype(BF16)

    def stack(x):
        return jnp.concatenate([jnp.where(lane_head == 0, x, 0.0),
                                jnp.where(lane_head == 1, x, 0.0)], axis=0)

    def twice(x):
        return jnp.concatenate([x, x], axis=0)


    def part(x, c, p):
        return x[c * L:(c + 1) * L, p * LANES:(p + 1) * LANES]

    cum_c = [_mm_split_lhs(tri, logw[c * L:(c + 1) * L]) for c in range(nc)]
    a_s, r_s, b16, k16, v16, bh16, kh16, decay = [], [], [], [], [], [], [], []
    for c, p in units:
        lw = part(logw, c, p)
        cum = cum_c[c][:, p * LANES:(p + 1) * LANES]
        tot = cum[L - 1:L, :]
        e_neg = jnp.exp(-cum)
        e_tail = jnp.exp(tot - cum)
        bv, kv = part(bvec, c, p), part(k2, c, p)
        a_s.append(stack(part(avec, c, p) * jnp.exp(cum - lw)))
        r_s.append(stack(part(rf, c, p) * jnp.exp(cum)))
        b16.append(twice((bv * e_neg).astype(BF16)))
        k16.append(twice((kv * e_neg).astype(BF16)))
        v16.append(stack(part(vf, c, p)).astype(BF16))
        bh16.append(stack(bv * e_tail).astype(BF16))
        kh16.append(stack(kv * e_tail).astype(BF16))
        decay.append(jnp.exp(tot))
    a16 = [x.astype(BF16) for x in a_s]
    bigs = [_mm_nt(jnp.concatenate([a16[c], r_s[c].astype(BF16)], axis=0),
                   jnp.concatenate([b16[c], k16[c]], axis=0)) for c in chunks]
    nd = [jnp.where(strict_blk, big[:S, :S], 0.0) for big in bigs]
    ne = [jnp.where(strict_off, big[:S, :S], 0.0) for big in bigs]
    a_ak = [jnp.where(strict, big[:S, S:], 0.0).astype(BF16) for big in bigs]
    a_br = [jnp.where(incl, big[S:, :S], 0.0).astype(BF16) for big in bigs]
    a_kr = [jnp.where(incl, big[S:, S:], 0.0).astype(BF16) for big in bigs]
    phase_b()

    akv = [_mm(a_ak[c], v16[c]) for c in chunks]
    pw = [_mm(x, x) for x in nd]
    sm = [nd[c] + pw[c] for c in chunks]
    pw = [x.astype(BF16) for x in pw]
    phase_b()
    for _ in range(2):
        both = [_mm(pw[c], jnp.concatenate([sm[c].astype(BF16), pw[c]], axis=1)) for c in chunks]
        sm = [sm[c] + both[c][:, :S] for c in chunks]
        pw = [both[c][:, S:].astype(BF16) for c in chunks]
        phase_b()
    dm = [sm[c] + _mm(pw[c], sm[c]) for c in chunks]
    phase_b()

    krv = [_mm(a_kr[c], v16[c]) for c in chunks]
    vkh = [_mm_tn(v16[c], kh16[c]) for c in chunks]
    phase_b(len(pending))
    for p in range(pairs):
        st_ref[p] = st[p]

    y = jnp.concatenate([jnp.concatenate(yp, axis=0) for yp in ys], axis=1)
    mean = head_sum(y) * (1.0 / RWKV_HEAD)
    d = y - mean
    var = head_sum(d * d) * (1.0 / RWKV_HEAD)
    yn = d * lax.rsqrt(var + RWKV_GN_EPS) * gnw_ref[...] + gnb_ref[...]
    y_ref[0] = ((yn + bonus_sc[...]) * g_sc[...]).astype(y_ref.dtype)

    f1 = [ne[c] + _mm(dm[c], ne[c]) for c in chunks]
    f12 = [f1[c] + _mm(f1[c], f1[c]) for c in chunks]
    gm = [f1[c] + _mm(f1[c], f12[c]) for c in chunks]
    tm1 = [_ipm(gm[c], dm[c]) for c in chunks]
    zmat = [jnp.concatenate([a_s[c], akv[c]], axis=1) for c in chunks]
    xmat = [zmat[c] + _mm(tm1[c], zmat[c]) for c in chunks]
    x16 = [x.astype(BF16) for x in xmat]
    zx = [_mm(a_br[c], x16[c]) for c in chunks]
    mu_t = [_mm_tn(x16[c], bh16[c]) for c in chunks]
    q = [r_s[c] + zx[c][:, :LANES] for c in chunks]
    y_intra = [zx[c][:, LANES:] + krv[c] for c in chunks]
    m_t = [mu_t[c][:LANES] for c in chunks]
    u_t = [mu_t[c][LANES:] + vkh[c] for c in chunks]

    bonus_sc[...] = bonus
    g_sc[...] = g
    for u in chunks:
        q_sc[u] = q[u].astype(BF16)
        m_sc[u] = m_t[u].astype(BF16)
        yi_sc[u] = y_intra[u]
        u_sc[u] = u_t[u]
        d_sc[u] = decay[u]


def _mm_split_lhs(a, b):
    hi = b.astype(BF16)
    lo = (b - hi.astype(F32)).astype(BF16)
    return (jnp.dot(a, hi, preferred_element_type=F32)
            + jnp.dot(a, lo, preferred_element_type=F32))


def _wkv7(h_a, mu, w0, a0, k_k, k_a, r_k, gn_w, gn_b, wd_pad, wa_pad, wg, tb, pairs):
    b, t, _ = h_a.shape
    width = pairs * LANES
    groups = D_RWKV // width
    nt = t // tb
    units = (tb // WKV_CHUNK) * pairs
    col = lambda off: (lambda bi, p, ti: (bi, jnp.minimum(ti, nt - 1), off + p))
    vec = lambda off: (lambda bi, p, ti: (0, off + p))
    lora_blk = (3 * D_RWKV) // (2 * LANES)
    in_specs = [
        pl.BlockSpec((1, tb, width), col(0)),
        pl.BlockSpec((1, tb, width), col(groups)),
        pl.BlockSpec((1, tb, width), col(2 * groups)),
        pl.BlockSpec((1, tb, 2 * LANES),
                     lambda bi, p, ti: (bi, jnp.minimum(ti, nt - 1), lora_blk)),
        pl.BlockSpec((1, width), vec(0)),
        pl.BlockSpec((1, width), vec(groups)),
        pl.BlockSpec((1, width), vec(2 * groups)),
        pl.BlockSpec((1, 2 * LANES), lambda bi, p, ti: (0, lora_blk)),
    ] + [pl.BlockSpec((1, width), vec(0))] * 7 + [
        pl.BlockSpec((LANES, width), vec(0)),
        pl.BlockSpec((LANES, width), vec(0)),
        pl.BlockSpec((LANES, width), vec(0)),
    ]
    return pl.pallas_call(
        functools.partial(_wkv_kernel, tb=tb, pairs=pairs),
        grid=(b, groups, nt + 1),
        in_specs=in_specs,
        out_specs=pl.BlockSpec((1, tb, width),
                               lambda bi, p, ti: (bi, jnp.maximum(ti - 1, 0), p)),
        out_shape=jax.ShapeDtypeStruct((b, t, D_RWKV), BF16),
        scratch_shapes=[
            pltpu.VMEM((pairs, LANES, LANES), F32),
            pltpu.VMEM((1, width), F32),
            pltpu.VMEM((1, width), F32),
            pltpu.VMEM((1, width), F32),
            pltpu.VMEM((1, 2 * LANES), F32),
            pltpu.VMEM((units, 2 * WKV_CHUNK, LANES), BF16),
            pltpu.VMEM((units, LANES, LANES), BF16),
            pltpu.VMEM((units, 2 * WKV_CHUNK, LANES), F32),
            pltpu.VMEM((units, LANES, LANES), F32),
            pltpu.VMEM((units, 1, LANES), F32),
            pltpu.VMEM((tb, width), F32),
            pltpu.VMEM((tb, width), F32),
        ],
        compiler_params=pltpu.CompilerParams(
            dimension_semantics=("parallel", "parallel", "arbitrary"),
            vmem_limit_bytes=VMEM_LIMIT),
        name="wkv7",
    )(h_a, h_a, h_a, h_a, mu, mu, mu, mu, w0, a0, k_k, k_a, r_k, gn_w, gn_b,
      wd_pad, wa_pad, wg)


def _ret_kernel(q_ref, k_ref, v_ref, g_ref, cos_ref, sin_ref, gnw_ref, y_ref, state_ref):
    C = RET_CHUNK
    half = RET_HEAD // 2
    hd = pl.program_id(1)

    @pl.when(pl.program_id(2) == 0)
    def _():
        state_ref[...] = jnp.zeros_like(state_ref)

    log_gamma = jnp.float32(math.log(1.0 - 2.0 ** -5.0))
    for i in range(1, RET_HEADS):
        log_gamma = jnp.where(hd == i, jnp.float32(math.log(1.0 - 2.0 ** (-5.0 - i))), log_gamma)

    cos = cos_ref[...]
    sin = sin_ref[...]

    def rotary(t):
        t1, t2 = t[:, :half], t[:, half:]
        return jnp.concatenate([t1 * cos - t2 * sin, t1 * sin + t2 * cos], axis=1)

    q_all = rotary(q_ref[0].astype(F32))
    k_all = rotary(k_ref[0].astype(F32)) * (RET_HEAD ** -0.5)
    v_all = v_ref[0]

    ri = lax.broadcasted_iota(jnp.int32, (C, C), 0)
    ci = lax.broadcasted_iota(jnp.int32, (C, C), 1)
    rel = (ri - ci).astype(F32)
    decay_mask = jnp.where(ri >= ci, jnp.exp(jnp.maximum(rel, 0.0) * log_gamma), 0.0)
    idx = lax.broadcasted_iota(jnp.int32, (C, 1), 0).astype(F32)
    q_decay = jnp.exp((idx + 1.0) * log_gamma)
    k_decay = jnp.exp((C - 1.0 - idx) * log_gamma)
    chunk_decay = jnp.exp(C * log_gamma)

    chunks = range(q_all.shape[0] // C)
    rows = lambda x, c: x[c * C:(c + 1) * C]
    q = [rows(q_all, c).astype(BF16) for c in chunks]
    k = [rows(k_all, c) for c in chunks]
    v = [rows(v_all, c) for c in chunks]
    scores = [_mm_nt(q[c], k[c]) * decay_mask for c in chunks]
    kv = [_mm_tn(k[c] * k_decay, v[c]) for c in chunks]
    intra = [_mm(scores[c], v[c]) for c in chunks]
    states = [state_ref[...]]
    for c in chunks:
        states.append(states[c] * chunk_decay + kv[c])
    state_ref[...] = states[-1]
    cross = [_mm(q[c], states[c]) * q_decay for c in chunks]

    y = jnp.concatenate([intra[c] + cross[c] for c in chunks], axis=0)
    mu = jnp.mean(y, axis=-1, keepdims=True)
    d = y - mu
    var = jnp.mean(d * d, axis=-1, keepdims=True)
    yn = d * lax.rsqrt(var + RET_GN_EPS) * gnw_ref[...]
    gate = g_ref[0].astype(F32)
    y_ref[0] = (yn * (gate * jax.nn.sigmoid(gate))).astype(y_ref.dtype)


def _retention(h_b, cos, sin, gn_w, tb):
    b, t, _ = h_b.shape
    col = lambda off: (lambda bi, h, n: (bi, n, off + h))
    return pl.pallas_call(
        _ret_kernel,
        grid=(b, RET_HEADS, t // tb),
        in_specs=[
            pl.BlockSpec((1, tb, RET_HEAD), col(0)),
            pl.BlockSpec((1, tb, RET_HEAD), col(RET_HEADS)),
            pl.BlockSpec((1, tb, RET_HEAD), col(2 * RET_HEADS)),
            pl.BlockSpec((1, tb, RET_HEAD), col(3 * RET_HEADS)),
            pl.BlockSpec((tb, RET_HEAD // 2), lambda bi, h, n: (n, 0)),
            pl.BlockSpec((tb, RET_HEAD // 2), lambda bi, h, n: (n, 0)),
            pl.BlockSpec((1, RET_HEAD), lambda bi, h, n: (0, h)),
        ],
        out_specs=pl.BlockSpec((1, tb, RET_HEAD), lambda bi, h, n: (bi, n, h)),
        out_shape=jax.ShapeDtypeStruct((b, t, D_RET), BF16),
        scratch_shapes=[pltpu.VMEM((RET_HEAD, RET_HEAD), F32)],
        compiler_params=pltpu.CompilerParams(
            dimension_semantics=("parallel", "parallel", "arbitrary")),
        name="retention",
    )(h_b, h_b, h_b, h_b, cos, sin, gn_w)


def _out_proj_kernel(x_ref, ya_ref, yb_ref, woa_ref, wob_ref, lnw_ref, lnb_ref, o_ref):
    parts = 4
    rows = x_ref.shape[0] // parts

    def mix(r):
        sl = pl.ds(r * rows, rows)
        return (jnp.dot(ya_ref[sl, :], woa_ref[...], preferred_element_type=F32)
                + jnp.dot(yb_ref[sl, :], wob_ref[...], preferred_element_type=F32))

    def finish(r, m):
        sl = pl.ds(r * rows, rows)
        o_ref[sl, :] = _layer_norm(ALPHA * x_ref[sl, :] + m, lnw_ref[...], lnb_ref[...])

    pending = mix(0)
    for r in range(1, parts):
        nxt = mix(r)
        finish(r - 1, pending)
        pending = nxt
    finish(parts - 1, pending)


def _out_proj(x2d, ya, yb, wo_a, wo_b, ln_w, ln_b, tm):
    m = x2d.shape[0]
    row = lambda w: pl.BlockSpec((tm, w), lambda i: (i, 0))
    full = lambda r, c: pl.BlockSpec((r, c), lambda i: (0, 0))
    return pl.pallas_call(
        _out_proj_kernel,
        grid=(m // tm,),
        in_specs=[row(D_MODEL), row(D_RWKV), row(D_RET), full(D_RWKV, D_MODEL),
                  full(D_RET, D_MODEL), full(1, D_MODEL), full(1, D_MODEL)],
        out_specs=row(D_MODEL),
        out_shape=jax.ShapeDtypeStruct((m, D_MODEL), F32),
        compiler_params=pltpu.CompilerParams(
            dimension_semantics=("parallel",), vmem_limit_bytes=VMEM_LIMIT),
        name="out_proj",
    )(x2d, ya, yb, wo_a, wo_b, ln_w, ln_b)


def _mlp_kernel(x_ref, wu_ref, wd_ref, lnw_ref, lnb_ref, o_ref):
    j = pl.program_id(1)

    def update():
        hid = jnp.dot(x_ref[...].astype(BF16), wu_ref[...], preferred_element_type=F32)
        hid = jnp.square(jnp.maximum(hid, 0.0))
        return jnp.dot(hid.astype(BF16), wd_ref[...], preferred_element_type=F32)

    @pl.when(j == 0)
    def _():
        o_ref[...] = ALPHA * x_ref[...] + update()

    @pl.when(j > 0)
    def _():
        o_ref[...] += update()

    @pl.when(j == pl.num_programs(1) - 1)
    def _():
        o_ref[...] = _layer_norm(o_ref[...], lnw_ref[...], lnb_ref[...])


def _mlp(x1, w_up, w_down, ln_w, ln_b, tm, tf):
    m = x1.shape[0]
    return pl.pallas_call(
        _mlp_kernel,
        grid=(m // tm, D_FF // tf),
        in_specs=[
            pl.BlockSpec((tm, D_MODEL), lambda i, j: (i, 0)),
            pl.BlockSpec((D_MODEL, tf), lambda i, j: (0, j)),
            pl.BlockSpec((tf, D_MODEL), lambda i, j: (j, 0)),
            pl.BlockSpec((1, D_MODEL), lambda i, j: (0, 0)),
            pl.BlockSpec((1, D_MODEL), lambda i, j: (0, 0)),
        ],
        out_specs=pl.BlockSpec((tm, D_MODEL), lambda i, j: (i, 0)),
        out_shape=jax.ShapeDtypeStruct((m, D_MODEL), F32),
        compiler_params=pltpu.CompilerParams(
            dimension_semantics=("parallel", "arbitrary"), vmem_limit_bytes=VMEM_LIMIT),
        name="mlp",
    )(x1, w_up, w_down, ln_w, ln_b)


def _layer(x, w_in, mu_shift, w0, w_lora_up, a0, a_lora_up, g_lora_up, k_k, k_a, r_k,
           rwkv_gn_w, rwkv_gn_b, ret_gn_w, w_o, ln1_w, ln1_b, w_up, w_down, ln2_w, ln2_b):
    b, t, _ = x.shape
    m = b * t
    x2d = x.reshape(m, D_MODEL)
    row = lambda p: p.reshape(1, -1)

    h_a, h_b = _in_proj(x2d, _prep_w_in(w_in), tm=min(IN_PROJ_ROWS, m))
    h_a = h_a.reshape(b, t, RWKV_WIDTH)
    h_b = h_b.reshape(b, t, RET_WIDTH)

    zeros = jnp.zeros((LORA_DECAY, D_RWKV), F32)
    wd_pad = jnp.concatenate([w_lora_up, zeros], axis=0).astype(BF16)
    wa_pad = jnp.concatenate([zeros, a_lora_up], axis=0).astype(BF16)
    y_a = _wkv7(h_a, row(mu_shift), row(w0), row(a0), row(k_k), row(k_a), row(r_k),
                row(rwkv_gn_w), row(rwkv_gn_b), wd_pad, wa_pad, g_lora_up.astype(BF16),
                tb=min(WKV_ROWS, t), pairs=WKV_PAIRS)

    cos, sin = _rope_table(t, rows=min(ROPE_ROWS, t))
    y_b = _retention(h_b, cos, sin, row(ret_gn_w), tb=min(RET_ROWS, t))

    w_o16 = w_o.astype(BF16)
    x1 = _out_proj(x2d, y_a.reshape(m, D_RWKV), y_b.reshape(m, D_RET), w_o16[:D_RWKV],
                   w_o16[D_RWKV:], row(ln1_w), row(ln1_b), tm=min(OUT_PROJ_ROWS, m))
    out = _mlp(x1, w_up.astype(BF16), w_down.astype(BF16), row(ln2_w), row(ln2_b),
               tm=min(MLP_ROWS, m), tf=MLP_COLS)
    return out.reshape(b, t, D_MODEL)


def kernel(x, w_in, mu_shift, w0, w_lora_up, a0, a_lora_up, g_lora_up, k_k, k_a, r_k,
           rwkv_gn_w, rwkv_gn_b, ret_gn_w, w_o, ln1_w, ln1_b, w_up, w_down, ln2_w, ln2_b):
    for l in range(DEPTH):
        x = _layer(x, w_in[l], mu_shift[l], w0[l], w_lora_up[l], a0[l], a_lora_up[l],
                   g_lora_up[l], k_k[l], k_a[l], r_k[l], rwkv_gn_w[l], rwkv_gn_b[l],
                   ret_gn_w[l], w_o[l], ln1_w[l], ln1_b[l], w_up[l], w_down[l],
                   ln2_w[l], ln2_b[l])
    return x
```
